```python
import jax, jax.numpy as jnp
from jax import lax
import numpy as np

D_MODEL = 1024
BATCH = 16
SEQ = 4096
DEPTH = 1

CHUNK = 64
CONV_CH = D_MODEL
CONV_WIDTH = 31
SG_WIDTH = D_MODEL
SG_HEADS = 8
SG_HEAD_DIM = SG_WIDTH // SG_HEADS
SG_BLOCK = 128
N_GROUPS = 4
EXPERTS_PER_GROUP = 8
N_EXPERTS = N_GROUPS * EXPERTS_PER_GROUP
TOP_K = 2
D_EXPERT = D_MODEL // 2
EXPERT_BLOCK = 256
IN_COLS = 2 * CONV_CH + 2 * SG_WIDTH + 2 * D_MODEL
EPS = 1e-6

kernel_name = "hybrid_conformer_gmlp_hmoe_block"


def rms_norm(x, g):
    xf = x.astype(jnp.float32)
    y = xf * lax.rsqrt(jnp.mean(xf * xf, axis=-1, keepdims=True) + EPS)
    return (y * g.astype(jnp.float32)).astype(x.dtype)


def layer_norm(x, g, b):
    xf = x.astype(jnp.float32)
    mu = jnp.mean(xf, axis=-1, keepdims=True)
    var = jnp.mean(jnp.square(xf - mu), axis=-1, keepdims=True)
    y = (xf - mu) * lax.rsqrt(var + EPS)
    return (y * g.astype(jnp.float32) + b.astype(jnp.float32)).astype(x.dtype)


def conformer_conv_branch(a, conv_w, conv_b, ln_g, ln_b, w_o, b_o):
    a1, a2 = jnp.split(a, 2, axis=-1)
    a = a1 * jax.nn.sigmoid(a2)
    kern = conv_w[:, None, :]
    y = lax.conv_general_dilated(
        a, kern, window_strides=(1,), padding=[(CONV_WIDTH - 1, 0)],
        dimension_numbers=("NWC", "WIO", "NWC"), feature_group_count=CONV_CH) + conv_b
    y = jax.nn.silu(layer_norm(y, ln_g, ln_b))
    return y @ w_o + b_o


def spatial_gating_branch(z, ln_g, ln_b, w_s, b_s, w_o, b_o):
    z = jax.nn.gelu(z, approximate=False)
    u, v = jnp.split(z, 2, axis=-1)
    v = layer_norm(v, ln_g, ln_b)
    bsz, seq, _ = v.shape
    n_blk = seq // SG_BLOCK
    v = v.reshape(bsz, n_blk, SG_BLOCK, SG_HEADS, SG_HEAD_DIM)
    blk = np.arange(SG_BLOCK) // CHUNK
    mask = jnp.asarray(blk[:, None] >= blk[None, :])
    w = w_s * mask[None].astype(w_s.dtype)
    sv = jnp.einsum("hqk,bnkhc->bnqhc", w, v) + b_s.T[:, :, None]
    s = u * sv.reshape(bsz, seq, SG_WIDTH)
    return s @ w_o + b_o


def hierarchical_moe(h, w_rg, b_rg, w_re, b_re, w_gate, w_up, w_down):
    bsz, seq, d = h.shape
    n_tok = bsz * seq
    xt = h.reshape(n_tok, d)
    g_logits = (xt @ w_rg + b_rg).astype(jnp.float32)
    g_prob = jax.nn.softmax(g_logits, axis=-1)
    g_sel = jnp.argmax(g_logits, axis=-1)
    g_w = jnp.take_along_axis(g_prob, g_sel[:, None], axis=-1)
    e_logits = (xt @ w_re + b_re).astype(jnp.float32).reshape(n_tok, N_GROUPS, EXPERTS_PER_GROUP)
    e_logits = jnp.take_along_axis(e_logits, g_sel[:, None, None], axis=1)[:, 0]
    top_v, top_i = lax.top_k(e_logits, TOP_K)
    e_w = jax.nn.softmax(top_v, axis=-1) * g_w
    expert_id = g_sel[:, None] * EXPERTS_PER_GROUP + top_i
    n_assign = n_tok * TOP_K
    flat_e = expert_id.reshape(-1)
    order = jnp.argsort(flat_e)
    sorted_e = flat_e[order]
    sorted_tok = order // TOP_K
    sorted_w = e_w.reshape(-1)[order]
    counts = jnp.bincount(flat_e, length=N_EXPERTS)
    starts = jnp.cumsum(counts) - counts
    padded = ((counts + EXPERT_BLOCK - 1) // EXPERT_BLOCK) * EXPERT_BLOCK
    pad_ends = jnp.cumsum(padded)
    pad_starts = pad_ends - padded
    dest = pad_starts[sorted_e] + jnp.arange(n_assign) - starts[sorted_e]
    n_blocks = -(-n_assign // EXPERT_BLOCK) + N_EXPERTS
    cap = n_blocks * EXPERT_BLOCK
    x_pad = jnp.zeros((cap, d), h.dtype).at[dest].set(xt[sorted_tok])
    block_e = jnp.minimum(
        jnp.searchsorted(pad_ends, jnp.arange(n_blocks) * EXPERT_BLOCK, side="right"), N_EXPERTS - 1)

    def run_block(args):
        xb, e = args
        return (jax.nn.silu(xb @ w_gate[e]) * (xb @ w_up[e])) @ w_down[e]

    y_pad = lax.map(run_block, (x_pad.reshape(n_blocks, EXPERT_BLOCK, d), block_e)).reshape(cap, d)
    y = y_pad[dest] * sorted_w[:, None].astype(h.dtype)
    out = jnp.zeros((n_tok, d), h.dtype).at[sorted_tok].add(y)
    return out.reshape(bsz, seq, d)


def setup_inputs(seed: int = 0) -> dict:
    key = jax.random.key(seed)
    ks = jax.random.split(key, 28)
    L, D = DEPTH, D_MODEL
    nrm = lambda k, shape, s: jax.random.normal(k, shape, jnp.float32) * s
    return {
        "x": nrm(ks[0], (BATCH, SEQ, D), 1.0),
        "norm_mix_g": 1.0 + nrm(ks[1], (L, D), 0.02),
        "w_in": nrm(ks[2], (L, D, IN_COLS), D ** -0.5),
        "b_in": nrm(ks[3], (L, IN_COLS), 0.02),
        "conv_w": nrm(ks[4], (L, CONV_WIDTH, CONV_CH), CONV_WIDTH ** -0.5),
        "conv_b": nrm(ks[5], (L, CONV_CH), 0.02),
        "conv_ln_g": 1.0 + nrm(ks[6], (L, CONV_CH), 0.02),
        "conv_ln_b": nrm(ks[7], (L, CONV_CH), 0.02),
        "w_conv_out": nrm(ks[8], (L, CONV_CH, D), CONV_CH ** -0.5),
        "b_conv_out": nrm(ks[9], (L, D), 0.02),
        "sg_ln_g": 1.0 + nrm(ks[10], (L, SG_WIDTH), 0.02),
        "sg_ln_b": nrm(ks[11], (L, SG_WIDTH), 0.02),
        "w_sg": nrm(ks[12], (L, SG_HEADS, SG_BLOCK, SG_BLOCK), SG_BLOCK ** -0.5),
        "b_sg": 1.0 + nrm(ks[13], (L, SG_HEADS, SG_BLOCK), 0.02),
        "w_sg_out": nrm(ks[14], (L, SG_WIDTH, D), SG_WIDTH ** -0.5),
        "b_sg_out": nrm(ks[15], (L, D), 0.02),
        "w_out": nrm(ks[16], (L, D, D), D ** -0.5),
        "b_out": nrm(ks[17], (L, D), 0.02),
        "norm_ffn_g": 1.0 + nrm(ks[18], (L, D), 0.02),
        "w_router_group": nrm(ks[19], (L, D, N_GROUPS), D ** -0.5),
        "b_router_group": nrm(ks[20], (L, N_GROUPS), 0.01),
        "w_router_expert": nrm(ks[21], (L, D, N_EXPERTS), D ** -0.5),
        "b_router_expert": nrm(ks[22], (L, N_EXPERTS), 0.01),
        "w_expert_gate": nrm(ks[23], (L, N_EXPERTS, D, D_EXPERT), D ** -0.5),
        "w_expert_up": nrm(ks[24], (L, N_EXPERTS, D, D_EXPERT), D ** -0.5),
        "w_expert_down": nrm(ks[25], (L, N_EXPERTS, D_EXPERT, D), D_EXPERT ** -0.5),
        "norm_final_g": 1.0 + nrm(ks[26], (D,), 0.02),
    }


def reference(x, norm_mix_g, w_in, b_in, conv_w, conv_b, conv_ln_g, conv_ln_b, w_conv_out, b_conv_out,
              sg_ln_g, sg_ln_b, w_sg, b_sg, w_sg_out, b_sg_out, w_out, b_out, norm_ffn_g,
              w_router_group, b_router_group, w_router_expert, b_router_expert,
              w_expert_gate, w_expert_up, w_expert_down, norm_final_g):
    for l in range(DEPTH):
        h = rms_norm(x, norm_mix_g[l])
        proj = h @ w_in[l] + b_in[l]
        a, z, gate_logits = jnp.split(proj, [2 * CONV_CH, 2 * CONV_CH + 2 * SG_WIDTH], axis=-1)
        y_a = conformer_conv_branch(a, conv_w[l], conv_b[l], conv_ln_g[l], conv_ln_b[l],
                                    w_conv_out[l], b_conv_out[l])
        y_b = spatial_gating_branch(z, sg_ln_g[l], sg_ln_b[l], w_sg[l], b_sg[l],
                                    w_sg_out[l], b_sg_out[l])
        g_a, g_b = jnp.split(jax.nn.sigmoid(gate_logits), 2, axis=-1)
        x = x + (g_a * y_a + g_b * y_b) @ w_out[l] + b_out[l]
        h = rms_norm(x, norm_ffn_g[l])
        x = x + hierarchical_moe(h, w_router_group[l], b_router_group[l], w_router_expert[l],
                                 b_router_expert[l], w_expert_gate[l], w_expert_up[l], w_expert_down[l])
    return rms_norm(x, norm_final_g)
```

```python
import functools

import jax
import jax.numpy as jnp
from jax import lax
from jax.experimental import pallas as pl
from jax.experimental.pallas import tpu as pltpu

F32 = jnp.float32
BF16 = jnp.bfloat16

D_MODEL = 1024
CONV_WIDTH = 31
SG_HEADS = 8
SG_BLOCK = 128
SG_CHUNK = 64
N_GROUPS = 4
EXPERTS_PER_GROUP = 8
N_EXPERTS = N_GROUPS * EXPERTS_PER_GROUP
D_EXPERT = D_MODEL // 2
EPS = 1e-6

SUBLANES = 8
LANES = 128
TS = 512
HALO = 32
ROWS = 16
CONV_ROWS = 64
NCOL = 512
RUN_ALIGN = SUBLANES
RL = -(-(2 * TS + N_EXPERTS * (RUN_ALIGN - 1)) // LANES) * LANES
NCH = RL // RUN_ALIGN
BR = 512
ROUTER_ROWS = 64
NEG = -1e30
VMEM_LIMIT = 60 * 1024 * 1024


def _rsqrt_mean_sq(xf):
    return lax.rsqrt(jnp.mean(xf * xf, axis=-1, keepdims=True) + EPS)


def _layer_norm(xf, g, b):
    mu = jnp.mean(xf, axis=-1, keepdims=True)
    xc = xf - mu
    var = jnp.mean(xc * xc, axis=-1, keepdims=True)
    return xc * lax.rsqrt(var + EPS) * g + b


def _gelu(z):
    return 0.5 * z * (1.0 + lax.erf(z * (0.5 ** 0.5)))


def _rows(i, n):
    return pl.ds(pl.multiple_of(i * n, n), n)


def _matmul_cols(a_ref, w_ref, c0, n, out_ref):
    for j in range(0, n, NCOL):
        out_ref[:, j:j + NCOL] = jnp.dot(a_ref[...], w_ref[:, c0 + j:c0 + j + NCOL],
                                         preferred_element_type=F32)


def _mixer_kernel(x_ref, gmix_ref, win_ref, bin_ref, cw_ref, cb_ref, clg_ref, clb_ref, wco_ref, bco_ref,
                  slg_ref, slb_ref, wsg_ref, bsgt_ref, wso_ref, bso_ref, wout_ref, bout_ref, gffn_ref,
                  wr_ref, br_ref,
                  x1_ref, h2_ref, rt_ref, cnt_ref,
                  h_s, p_s, a_s, c_s, ya_s, yb_s, u_s, v_s):
    d = D_MODEL
    s_idx = pl.program_id(1)

    def norm_body(i, carry):
        r = _rows(i, ROWS)
        xf = x_ref[r, :]
        h_s[r, :] = (xf * _rsqrt_mean_sq(xf) * gmix_ref[...]).astype(BF16)
        return carry
    lax.fori_loop(0, TS // ROWS, norm_body, 0)

    @pl.when(s_idx == 0)
    def _():
        a_s[0:HALO, :] = jnp.zeros((HALO, d), F32)

    @pl.when(s_idx > 0)
    def _():
        a_s[0:HALO, :] = a_s[TS:TS + HALO, :]

    _matmul_cols(h_s, win_ref, 0, 2 * d, p_s)

    def glu_body(i, carry):
        r = _rows(i, ROWS)
        a1 = p_s[r, 0:d] + bin_ref[:, 0:d]
        a2 = p_s[r, d:2 * d] + bin_ref[:, d:2 * d]
        a_s[pl.ds(pl.multiple_of(i * ROWS + HALO, ROWS), ROWS), :] = a1 * jax.nn.sigmoid(a2)
        return carry
    lax.fori_loop(0, TS // ROWS, glu_body, 0)

    base_off = HALO - (CONV_WIDTH - 1)
    win_rows = CONV_ROWS + HALO

    def conv_body(i, carry):
        r0 = pl.multiple_of(i * CONV_ROWS, CONV_ROWS)
        for c0 in range(0, d, LANES):
            win = a_s[pl.ds(r0, win_rows), c0:c0 + LANES]
            acc = jnp.broadcast_to(cb_ref[:, c0:c0 + LANES], (CONV_ROWS, LANES))
            for j in range(SUBLANES):
                sh = win if j == 0 else pltpu.roll(win, win_rows - j, axis=0)
                for q in range(HALO // SUBLANES + 1):
                    k = q * SUBLANES + j - base_off
                    if 0 <= k < CONV_WIDTH:
                        acc = acc + cw_ref[k:k + 1, c0:c0 + LANES] * sh[q * SUBLANES:q * SUBLANES + CONV_ROWS, :]
            p_s[pl.ds(r0, CONV_ROWS), c0:c0 + LANES] = acc
        return carry
    lax.fori_loop(0, TS // CONV_ROWS, conv_body, 0)

    def conv_ln_body(i, carry):
        r = _rows(i, ROWS)
        y = _layer_norm(p_s[r, 0:d], clg_ref[...], clb_ref[...])
        c_s[r, :] = (y * jax.nn.sigmoid(y)).astype(BF16)
        return carry
    lax.fori_loop(0, TS // ROWS, conv_ln_body, 0)

    _matmul_cols(c_s, wco_ref, 0, d, ya_s)

    _matmul_cols(h_s, win_ref, 2 * d, 2 * d, p_s)

    def gelu_body(i, carry):
        r = _rows(i, ROWS)
        z1 = p_s[r, 0:d] + bin_ref[:, 2 * d:3 * d]
        z2 = p_s[r, d:2 * d] + bin_ref[:, 3 * d:4 * d]
        u_s[r, :] = _gelu(z1)
        v_s[r, :] = _layer_norm(_gelu(z2), slg_ref[...], slb_ref[...]).astype(BF16)
        return carry
    lax.fori_loop(0, TS // ROWS, gelu_body, 0)

    qi = lax.broadcasted_iota(jnp.int32, (SG_BLOCK, SG_BLOCK), 0) // SG_CHUNK
    ki = lax.broadcasted_iota(jnp.int32, (SG_BLOCK, SG_BLOCK), 1) // SG_CHUNK
    causal = qi >= ki
    for hh in range(SG_HEADS):
        wm = jnp.where(causal, wsg_ref[hh], 0.0).astype(BF16)
        cs = slice(hh * SG_BLOCK, (hh + 1) * SG_BLOCK)
        for blk in range(TS // SG_BLOCK):
            rs = slice(blk * SG_BLOCK, (blk + 1) * SG_BLOCK)
            sv = jnp.dot(wm, v_s[rs, cs], preferred_element_type=F32) + bsgt_ref[:, hh:hh + 1]
            c_s[rs, cs] = (u_s[rs, cs] * sv).astype(BF16)

    _matmul_cols(c_s, wso_ref, 0, d, yb_s)

    _matmul_cols(h_s, win_ref, 4 * d, 2 * d, p_s)

    def gate_body(i, carry):
        r = _rows(i, ROWS)
        ga = jax.nn.sigmoid(p_s[r, 0:d] + bin_ref[:, 4 * d:5 * d])
        gb = jax.nn.sigmoid(p_s[r, d:2 * d] + bin_ref[:, 5 * d:6 * d])
        m = ga * (ya_s[r, :] + bco_ref[...]) + gb * (yb_s[r, :] + bso_ref[...])
        c_s[r, :] = m.astype(BF16)
        return carry
    lax.fori_loop(0, TS // ROWS, gate_body, 0)

    _matmul_cols(c_s, wout_ref, 0, d, ya_s)

    def resid_body(i, carry):
        r = _rows(i, ROWS)
        x1 = x_ref[r, :] + ya_s[r, :] + bout_ref[...]
        x1_ref[r, :] = x1
        h2_ref[r, :] = (x1 * _rsqrt_mean_sq(x1) * gffn_ref[...]).astype(BF16)
        return carry
    lax.fori_loop(0, TS // ROWS, resid_body, 0)

    lt = lax.dot_general(wr_ref[...], h2_ref[...], (((1,), (1,)), ((), ())),
                         preferred_element_type=F32) + br_ref[...]
    row8 = lax.broadcasted_iota(jnp.int32, (EXPERTS_PER_GROUP, TS), 0)
    gl = jnp.where(row8 < N_GROUPS, lt[N_EXPERTS:N_EXPERTS + 8], NEG)
    gmax = jnp.max(gl, axis=0, keepdims=True)
    gsel = jnp.min(jnp.where(gl == gmax, row8, 8), axis=0, keepdims=True)
    gw = 1.0 / jnp.sum(jnp.exp(gl - gmax), axis=0, keepdims=True)
    el = jnp.zeros((EXPERTS_PER_GROUP, TS), F32)
    for g in range(N_GROUPS):
        el = jnp.where(gsel == g, lt[g * EXPERTS_PER_GROUP:(g + 1) * EXPERTS_PER_GROUP], el)
    m1 = jnp.max(el, axis=0, keepdims=True)
    i1 = jnp.min(jnp.where(el == m1, row8, 8), axis=0, keepdims=True)
    el2 = jnp.where(row8 == i1, NEG, el)
    m2 = jnp.max(el2, axis=0, keepdims=True)
    i2 = jnp.min(jnp.where(el2 == m2, row8, 8), axis=0, keepdims=True)
    t = jnp.exp(m2 - m1)
    w0 = gw / (1.0 + t)
    w1 = gw * t / (1.0 + t)
    e0 = gsel * EXPERTS_PER_GROUP + i1
    e1 = gsel * EXPERTS_PER_GROUP + i2

    row32 = lax.broadcasted_iota(jnp.int32, (N_EXPERTS, TS), 0)
    oh0 = (row32 == e0).astype(F32)
    oh1 = (row32 == e1).astype(F32)
    c0 = jnp.sum(oh0, axis=1, keepdims=True)
    c1 = jnp.sum(oh1, axis=1, keepdims=True)
    padded = jnp.floor((c0 + c1 + (RUN_ALIGN - 1)) * (1.0 / RUN_ALIGN)) * RUN_ALIGN
    tri = (lax.broadcasted_iota(jnp.int32, (N_EXPERTS, N_EXPERTS), 0)
           > lax.broadcasted_iota(jnp.int32, (N_EXPERTS, N_EXPERTS), 1)).astype(BF16)
    units = jnp.broadcast_to(padded * (1.0 / RUN_ALIGN), (N_EXPERTS, LANES)).astype(BF16)
    lstart = RUN_ALIGN * jnp.dot(tri, units, preferred_element_type=F32)[:, 0:1]
    before = (lax.broadcasted_iota(jnp.int32, (TS, TS), 0)
              < lax.broadcasted_iota(jnp.int32, (TS, TS), 1)).astype(BF16)
    rank = jnp.dot(jnp.concatenate([oh0, oh1], axis=0).astype(BF16), before,
                   preferred_element_type=F32)
    pos0 = jnp.sum(oh0 * (lstart + rank[0:N_EXPERTS]), axis=0, keepdims=True)
    pos1 = jnp.sum(oh1 * (lstart + c0 + rank[N_EXPERTS:]), axis=0, keepdims=True)
    rt_ref[...] = jnp.concatenate(
        [pos0, pos1, w0, w1, e0.astype(F32), e1.astype(F32), jnp.zeros((2, TS), F32)], axis=0)
    cnt_ref[...] = jnp.broadcast_to(padded, (N_EXPERTS, LANES))


def _const_spec(shape):
    nd = len(shape)
    return pl.BlockSpec(shape, lambda b, s: (0,) * nd, pipeline_mode=pl.Buffered(1))


def _mixer(x, consts):
    bsz, seq, d = x.shape
    ns = seq // TS
    nt = bsz * ns
    in_specs = [pl.BlockSpec((None, TS, d), lambda b, s: (b, s, 0))]
    in_specs += [_const_spec(c.shape) for c in consts]
    out_shape = (
        jax.ShapeDtypeStruct((bsz, seq, d), F32),
        jax.ShapeDtypeStruct((bsz, seq, d), BF16),
        jax.ShapeDtypeStruct((nt, SUBLANES, TS), F32),
        jax.ShapeDtypeStruct((nt, N_EXPERTS, LANES), F32),
    )
    out_specs = (
        pl.BlockSpec((None, TS, d), lambda b, s: (b, s, 0)),
        pl.BlockSpec((None, TS, d), lambda b, s: (b, s, 0)),
        pl.BlockSpec((None, SUBLANES, TS), lambda b, s: (b * ns + s, 0, 0)),
        pl.BlockSpec((None, N_EXPERTS, LANES), lambda b, s: (b * ns + s, 0, 0)),
    )
    scratch = [
        pltpu.VMEM((TS, d), BF16),
        pltpu.VMEM((TS, 2 * d), F32),
        pltpu.VMEM((TS + HALO, d), F32),
        pltpu.VMEM((TS, d), BF16),
        pltpu.VMEM((TS, d), F32),
        pltpu.VMEM((TS, d), F32),
        pltpu.VMEM((TS, d), F32),
        pltpu.VMEM((TS, d), BF16),
    ]
    return pl.pallas_call(
        _mixer_kernel,
        grid=(bsz, ns),
        in_specs=in_specs,
        out_specs=out_specs,
        out_shape=out_shape,
        scratch_shapes=scratch,
        compiler_params=pltpu.CompilerParams(
            dimension_semantics=("arbitrary", "arbitrary"), vmem_limit_bytes=VMEM_LIMIT),
        name="mixer",
    )(x, *consts)


def _chunk_copy(src, dst, sem, s0, d0):
    return pltpu.make_async_copy(src.at[pl.ds(s0, RUN_ALIGN)], dst.at[pl.ds(d0, RUN_ALIGN)], sem)


def _dispatch_kernel(gtab_ref, nch_ref, ztab_ref, nz_ref, nv_ref, rt_ref, h2_ref, xpad_ref, buf, zbuf, sem, zsem):
    i = pl.program_id(0)
    n = nch_ref[i]
    pos0 = rt_ref[0:1, :]
    pos1 = rt_ref[1:2, :]
    slot = lax.broadcasted_iota(jnp.int32, (RL, TS), 0).astype(F32)
    onehot = ((slot == pos0) | (slot == pos1)).astype(BF16)
    buf[...] = jnp.dot(onehot, h2_ref[...], preferred_element_type=F32)

    def issue(j, carry):
        g = pl.multiple_of(gtab_ref[i * NCH + j], RUN_ALIGN)
        _chunk_copy(buf, xpad_ref, sem, pl.multiple_of(j * RUN_ALIGN, RUN_ALIGN), g).start()
        return carry
    lax.fori_loop(0, n, issue, 0)

    @pl.when(i == pl.num_programs(0) - 1)
    def _():
        zbuf[...] = jnp.zeros((BR, D_MODEL), F32)
        nzt = nz_ref[0]
        nb = xpad_ref.shape[0] // BR

        def block_copy(b):
            return pltpu.make_async_copy(zbuf, xpad_ref.at[pl.ds(pl.multiple_of(b * BR, BR), BR)], zsem)

        def zissue(j, carry):
            g = pl.multiple_of(ztab_ref[j], RUN_ALIGN)
            _chunk_copy(zbuf, xpad_ref, zsem, 0, g).start()
            return carry
        lax.fori_loop(0, nzt, zissue, 0)

        def bissue(b, carry):
            block_copy(b).start()
            return carry
        lax.fori_loop(nv_ref[0], nb, bissue, 0)

        def zwait(j, carry):
            _chunk_copy(zbuf, xpad_ref, zsem, 0, 0).wait()
            return carry
        lax.fori_loop(0, nzt, zwait, 0)

        def bwait(b, carry):
            block_copy(0).wait()
            return carry
        lax.fori_loop(nv_ref[0], nb, bwait, 0)

    def wait(j, carry):
        _chunk_copy(buf, xpad_ref, sem, 0, 0).wait()
        return carry
    lax.fori_loop(0, n, wait, 0)


def _dispatch(gtab, nch, ztab, nz, nvalid, rt, h2, cap):
    nt = rt.shape[0]
    d = h2.shape[-1]
    grid_spec = pltpu.PrefetchScalarGridSpec(
        num_scalar_prefetch=5,
        grid=(nt,),
        in_specs=[
            pl.BlockSpec((None, SUBLANES, TS), lambda i, *_: (i, 0, 0)),
            pl.BlockSpec((TS, d), lambda i, *_: (i, 0)),
        ],
        out_specs=pl.BlockSpec(memory_space=pl.ANY),
        scratch_shapes=[
            pltpu.VMEM((RL, d), F32),
            pltpu.VMEM((BR, d), F32),
            pltpu.SemaphoreType.DMA,
            pltpu.SemaphoreType.DMA,
        ],
    )
    return pl.pallas_call(
        _dispatch_kernel,
        grid_spec=grid_spec,
        out_shape=jax.ShapeDtypeStruct((cap, d), F32),
        compiler_params=pltpu.CompilerParams(
            dimension_semantics=("arbitrary",), vmem_limit_bytes=VMEM_LIMIT),
        name="dispatch",
    )(gtab, nch, ztab, nz, nvalid, rt, h2)


def _expert_kernel(be_ref, nv_ref, x_ref, wg_ref, wu_ref, wd_ref, y_ref):
    @pl.when(pl.program_id(0) >= nv_ref[0])
    def _():
        y_ref[...] = jnp.zeros(y_ref.shape, F32)

    @pl.when(pl.program_id(0) < nv_ref[0])
    def _():
        xb = x_ref[...].astype(BF16)
        g = jnp.dot(xb, wg_ref[...], preferred_element_type=F32)
        u = jnp.dot(xb, wu_ref[...], preferred_element_type=F32)
        hid = (g * jax.nn.sigmoid(g) * u).astype(BF16)
        y_ref[...] = jnp.dot(hid, wd_ref[...], preferred_element_type=F32)


def _experts(block_e, nvalid, xpad, wg, wu, wd):
    cap, d = xpad.shape
    nb = cap // BR

    def row_map(b, be, nv):
        return (jnp.minimum(b, nv[0] - 1), 0)

    def w_map(b, be, nv):
        return (be[b], 0, 0)

    grid_spec = pltpu.PrefetchScalarGridSpec(
        num_scalar_prefetch=2,
        grid=(nb,),
        in_specs=[
            pl.BlockSpec((BR, d), row_map),
            pl.BlockSpec((None, d, D_EXPERT), w_map),
            pl.BlockSpec((None, d, D_EXPERT), w_map),
            pl.BlockSpec((None, D_EXPERT, d), w_map),
        ],
        out_specs=pl.BlockSpec((BR, d), lambda b, be, nv: (b, 0)),
    )
    return pl.pallas_call(
        _expert_kernel,
        grid_spec=grid_spec,
        out_shape=jax.ShapeDtypeStruct((cap, d), F32),
        compiler_params=pltpu.CompilerParams(
            dimension_semantics=("arbitrary",), vmem_limit_bytes=VMEM_LIMIT),
        name="experts",
    )(block_e, nvalid, xpad, wg, wu, wd)


def _combine_kernel(gtab_ref, nch_ref, rt_ref, x1_ref, gfin_ref, ypad_ref, out_ref, buf, sem):
    i = pl.program_id(0)
    n = nch_ref[i]

    def issue(j, carry):
        g = pl.multiple_of(gtab_ref[i * NCH + j], RUN_ALIGN)
        _chunk_copy(ypad_ref, buf, sem, g, pl.multiple_of(j * RUN_ALIGN, RUN_ALIGN)).start()
        return carry
    lax.fori_loop(0, n, issue, 0)

    def zero_body(j, carry):
        buf[pl.ds(pl.multiple_of(j * RUN_ALIGN, RUN_ALIGN), RUN_ALIGN), :] = jnp.zeros((RUN_ALIGN, D_MODEL), F32)
        return carry
    lax.fori_loop(n, NCH, zero_body, 0)

    pos0 = rt_ref[0:1, :]
    pos1 = rt_ref[1:2, :]
    w0 = rt_ref[2:3, :]
    w1 = rt_ref[3:4, :]
    slot = lax.broadcasted_iota(jnp.int32, (RL, TS), 0).astype(F32)
    wslot = jnp.sum(jnp.where(slot == pos0, w0, 0.0) + jnp.where(slot == pos1, w1, 0.0),
                    axis=1, keepdims=True)
    rt_cols = jnp.concatenate([rt_ref[...], jnp.zeros((LANES - SUBLANES, TS), F32)], axis=0).T
    p0c = rt_cols[:, 0:1]
    p1c = rt_cols[:, 1:2]
    lane_slot = lax.broadcasted_iota(jnp.int32, (TS, RL), 1).astype(F32)
    unsort = ((lane_slot == p0c) | (lane_slot == p1c)).astype(BF16)

    def wait(j, carry):
        _chunk_copy(ypad_ref, buf, sem, 0, 0).wait()
        return carry
    lax.fori_loop(0, n, wait, 0)

    yw = (buf[...] * wslot).astype(BF16)
    x2 = x1_ref[...] + jnp.dot(unsort, yw, preferred_element_type=F32)
    out_ref[...] = x2 * _rsqrt_mean_sq(x2) * gfin_ref[...]


def _combine(gtab, nch, rt, x1, gfin, ypad):
    nt = rt.shape[0]
    d = x1.shape[-1]
    grid_spec = pltpu.PrefetchScalarGridSpec(
        num_scalar_prefetch=2,
        grid=(nt,),
        in_specs=[
            pl.BlockSpec((None, SUBLANES, TS), lambda i, *_: (i, 0, 0)),
            pl.BlockSpec((TS, d), lambda i, *_: (i, 0)),
            pl.BlockSpec((1, d), lambda i, *_: (0, 0)),
            pl.BlockSpec(memory_space=pl.ANY),
        ],
        out_specs=pl.BlockSpec((TS, d), lambda i, *_: (i, 0)),
        scratch_shapes=[pltpu.VMEM((RL, d), F32), pltpu.SemaphoreType.DMA],
    )
    return pl.pallas_call(
        _combine_kernel,
        grid_spec=grid_spec,
        out_shape=jax.ShapeDtypeStruct((nt * TS, d), F32),
        compiler_params=pltpu.CompilerParams(
            dimension_semantics=("arbitrary",), vmem_limit_bytes=VMEM_LIMIT),
        name="combine",
    )(gtab, nch, rt, x1, gfin, ypad)


def _run_tables(cnt, cap):
    nt = cnt.shape[0]
    pc = cnt[:, :, 0].astype(jnp.int32)
    tot = jnp.sum(pc, axis=0)
    reg = ((tot + BR - 1) // BR) * BR
    reg_end = jnp.cumsum(reg)
    reg_start = reg_end - reg
    goff = reg_start[None, :] + jnp.cumsum(pc, axis=0) - pc
    lend = jnp.cumsum(pc, axis=1)
    lstart = lend - pc
    nch = (lend[:, -1] // RUN_ALIGN).astype(jnp.int32)
    local = jnp.arange(NCH, dtype=jnp.int32) * RUN_ALIGN
    e_of = jnp.sum(lend[:, None, :] <= local[None, :, None], axis=-1)
    e_of = jnp.minimum(e_of, N_EXPERTS - 1)
    gtab = (jnp.take_along_axis(goff, e_of, axis=1) + local[None, :]
            - jnp.take_along_axis(lstart, e_of, axis=1))
    gtab = jnp.where(jnp.arange(NCH)[None, :] < nch[:, None], gtab, 0).astype(jnp.int32).reshape(-1)
    nzmax = BR // RUN_ALIGN
    zrow = (reg_start + tot)[:, None] + jnp.arange(nzmax, dtype=jnp.int32)[None, :] * RUN_ALIGN
    zvalid = zrow < reg_end[:, None]
    zorder = jnp.argsort(~zvalid.reshape(-1), stable=True)
    ztab = jnp.where(zvalid, zrow, 0).reshape(-1)[zorder].astype(jnp.int32)
    nz = jnp.sum(zvalid).astype(jnp.int32).reshape(1)
    nb = cap // BR
    block_e = jnp.minimum(
        jnp.sum(reg_end[None, :] <= (jnp.arange(nb, dtype=jnp.int32) * BR)[:, None], axis=-1),
        N_EXPERTS - 1).astype(jnp.int32)
    nvalid = (reg_end[-1] // BR).astype(jnp.int32).reshape(1)
    return gtab, nch, ztab, nz, block_e, nvalid


def kernel(x, norm_mix_g, w_in, b_in, conv_w, conv_b, conv_ln_g, conv_ln_b, w_conv_out, b_conv_out, sg_ln_g, sg_ln_b, w_sg, b_sg, w_sg_out, b_sg_out, w_out, b_out, norm_ffn_g, w_router_group, b_router_group, w_router_expert, b_router_expert, w_expert_gate, w_expert_up, w_expert_down, norm_final_g):
    bsz, seq, d = x.shape
    assert d == D_MODEL and seq % TS == 0 and w_in.shape[0] == 1
    nt = bsz * seq // TS
    n_tok = bsz * seq
    cap = -(-(2 * n_tok + nt * N_EXPERTS * (RUN_ALIGN - 1) + N_EXPERTS * (BR - RUN_ALIGN)) // BR) * BR

    row = lambda v: v.reshape(1, -1).astype(F32)
    wr = jnp.zeros((ROUTER_ROWS, d), F32)
    wr = wr.at[0:N_EXPERTS].set(w_router_expert[0].T).at[N_EXPERTS:N_EXPERTS + N_GROUPS].set(w_router_group[0].T)
    br = jnp.zeros((ROUTER_ROWS, 1), F32)
    br = br.at[0:N_EXPERTS, 0].set(b_router_expert[0]).at[N_EXPERTS:N_EXPERTS + N_GROUPS, 0].set(b_router_group[0])
    consts = [
        row(norm_mix_g[0]), w_in[0].astype(BF16), row(b_in[0]),
        conv_w[0], row(conv_b[0]), row(conv_ln_g[0]), row(conv_ln_b[0]),
        w_conv_out[0].astype(BF16), row(b_conv_out[0]),
        row(sg_ln_g[0]), row(sg_ln_b[0]), w_sg[0], b_sg[0].T,
        w_sg_out[0].astype(BF16), row(b_sg_out[0]),
        w_out[0].astype(BF16), row(b_out[0]), row(norm_ffn_g[0]),
        wr.astype(BF16), br,
    ]
    x1, h2, rt, cnt = _mixer(x, consts)
    gtab, nch, ztab, nz, block_e, nvalid = _run_tables(cnt, cap)
    xpad = _dispatch(gtab, nch, ztab, nz, nvalid, rt, h2.reshape(n_tok, d), cap)
    ypad = _experts(block_e, nvalid, xpad, w_expert_gate[0].astype(BF16), w_expert_up[0].astype(BF16),
                    w_expert_down[0].astype(BF16))
    out = _combine(gtab, nch, rt, x1.reshape(n_tok, d), row(norm_final_g), ypad)
    return out.reshape(bsz, seq, d)
```

```python
import functools

import jax
import jax.numpy as jnp
from jax import lax
from jax.experimental import pallas as pl
from jax.experimental.pallas import tpu as pltpu

F32 = jnp.float32
BF16 = jnp.bfloat16

D_MODEL = 1024
CONV_WIDTH = 31
SG_HEADS = 8
SG_BLOCK = 128
SG_CHUNK = 64
N_GROUPS = 4
EXPERTS_PER_GROUP = 8
N_EXPERTS = N_GROUPS * EXPERTS_PER_GROUP
D_EXPERT = D_MODEL // 2
EPS = 1e-6

SUBLANES = 8
LANES = 128
TS = 512
HALO = 32
ROWS = 16
STAT_ROWS = 128
CONV_ROWS = 64
NCOL = 512
RUN_ALIGN = SUBLANES
RL = -(-(2 * TS + N_EXPERTS * (RUN_ALIGN - 1)) // LANES) * LANES
NCH = RL // RUN_ALIGN
BR = 512
ROUTER_ROWS = 64
NEG = -1e30
VMEM_LIMIT = 60 * 1024 * 1024


def _rsqrt_mean_sq(xf):
    return lax.rsqrt(jnp.mean(xf * xf, axis=-1, keepdims=True) + EPS)


def _layer_norm(xf, g, b):
    mu = jnp.mean(xf, axis=-1, keepdims=True)
    xc = xf - mu
    var = jnp.mean(xc * xc, axis=-1, keepdims=True)
    return xc * lax.rsqrt(var + EPS) * g + b


def _gelu(z):
    return 0.5 * z * (1.0 + lax.erf(z * (0.5 ** 0.5)))


def _rows(i, n):
    return pl.ds(pl.multiple_of(i * n, n), n)


def _matmul_cols(a_ref, w_ref, c0, n, out_ref):
    for j in range(0, n, NCOL):
        out_ref[:, j:j + NCOL] = jnp.dot(a_ref[...], w_ref[:, c0 + j:c0 + j + NCOL],
                                         preferred_element_type=F32)


def _mixer_kernel(x_ref, gmix_ref, win_ref, bin_ref, cw_ref, cb_ref, clg_ref, clb_ref, wco_ref, bco_ref,
                  slg_ref, slb_ref, wsg_ref, bsgt_ref, wso_ref, bso_ref, wout_ref, bout_ref, gffn_ref,
                  wr_ref, br_ref,
                  x1_ref, h2_ref, rt_ref, cnt_ref,
                  h_s, p_s, p2_s, a_s, c_s, ya_s, yb_s, u_s, v_s):
    d = D_MODEL
    s_idx = pl.program_id(1)

    def norm_body(i, carry):
        r = _rows(i, STAT_ROWS)
        xf = x_ref[r, :]
        h_s[r, :] = (xf * _rsqrt_mean_sq(xf) * gmix_ref[...]).astype(BF16)
        return carry
    lax.fori_loop(0, TS // STAT_ROWS, norm_body, 0)

    @pl.when(s_idx == 0)
    def _():
        a_s[0:HALO, :] = jnp.zeros((HALO, d), F32)

    @pl.when(s_idx > 0)
    def _():
        a_s[0:HALO, :] = a_s[TS:TS + HALO, :]

    for j in range(2 * d // NCOL):
        p_s[:, j * NCOL:(j + 1) * NCOL] = jnp.dot(h_s[...], win_ref[j], preferred_element_type=F32)

    def glu_body(i, carry):
        r = _rows(i, ROWS)
        a1 = p_s[r, 0:d] + bin_ref[:, 0:d]
        a2 = p_s[r, d:2 * d] + bin_ref[:, d:2 * d]
        a_s[pl.ds(pl.multiple_of(i * ROWS + HALO, ROWS), ROWS), :] = a1 * jax.nn.sigmoid(a2)
        return carry
    lax.fori_loop(0, TS // ROWS, glu_body, 0)

    base_off = HALO - (CONV_WIDTH - 1)
    win_rows = CONV_ROWS + HALO

    def conv_body(i, carry):
        r0 = pl.multiple_of(i * CONV_ROWS, CONV_ROWS)
        for c0 in range(0, d, LANES):
            win = a_s[pl.ds(r0, win_rows), c0:c0 + LANES]
            acc = jnp.broadcast_to(cb_ref[:, c0:c0 + LANES], (CONV_ROWS, LANES))
            for j in range(SUBLANES):
                sh = win if j == 0 else pltpu.roll(win, win_rows - j, axis=0)
                for q in range(HALO // SUBLANES + 1):
                    k = q * SUBLANES + j - base_off
                    if 0 <= k < CONV_WIDTH:
                        acc = acc + cw_ref[k:k + 1, c0:c0 + LANES] * sh[q * SUBLANES:q * SUBLANES + CONV_ROWS, :]
            p_s[pl.ds(r0, CONV_ROWS), c0:c0 + LANES] = acc
        p2_s[i] = jnp.dot(h_s[...], win_ref[2 * d // NCOL + i], preferred_element_type=F32)
        return carry
    lax.fori_loop(0, TS // CONV_ROWS, conv_body, 0)

    def conv_ln_body(i, carry):
        r = _rows(i, STAT_ROWS)
        y = _layer_norm(p_s[r, 0:d], clg_ref[...], clb_ref[...])
        c_s[r, :] = (y * jax.nn.sigmoid(y)).astype(BF16)
        return carry
    lax.fori_loop(0, TS // STAT_ROWS, conv_ln_body, 0)

    _matmul_cols(c_s, wco_ref, 0, d, ya_s)

    def proj2(r, k):
        cpd = d // NCOL
        cols = jnp.concatenate([p2_s[k * cpd + j, r, :] for j in range(cpd)], axis=1)
        return cols + bin_ref[:, (2 + k) * d:(3 + k) * d]

    def gelu_body(i, carry):
        r = _rows(i, STAT_ROWS)
        z1 = proj2(r, 0)
        z2 = proj2(r, 1)
        u_s[r, :] = _gelu(z1)
        v_s[r, :] = _layer_norm(_gelu(z2), slg_ref[...], slb_ref[...]).astype(BF16)
        return carry
    lax.fori_loop(0, TS // STAT_ROWS, gelu_body, 0)

    qi = lax.broadcasted_iota(jnp.int32, (SG_BLOCK, SG_BLOCK), 0) // SG_CHUNK
    ki = lax.broadcasted_iota(jnp.int32, (SG_BLOCK, SG_BLOCK), 1) // SG_CHUNK
    causal = qi >= ki
    for hh in range(SG_HEADS):
        wm = jnp.where(causal, wsg_ref[hh], 0.0).astype(BF16)
        cs = slice(hh * SG_BLOCK, (hh + 1) * SG_BLOCK)
        for blk in range(TS // SG_BLOCK):
            rs = slice(blk * SG_BLOCK, (blk + 1) * SG_BLOCK)
            sv = jnp.dot(wm, v_s[rs, cs], preferred_element_type=F32) + bsgt_ref[:, hh:hh + 1]
            c_s[rs, cs] = (u_s[rs, cs] * sv).astype(BF16)

    _matmul_cols(c_s, wso_ref, 0, d, yb_s)

    def gate_body(i, carry):
        r = _rows(i, ROWS)
        ga = jax.nn.sigmoid(proj2(r, 2))
        gb = jax.nn.sigmoid(proj2(r, 3))
        m = ga * (ya_s[r, :] + bco_ref[...]) + gb * (yb_s[r, :] + bso_ref[...])
        c_s[r, :] = m.astype(BF16)
        return carry
    lax.fori_loop(0, TS // ROWS, gate_body, 0)

    _matmul_cols(c_s, wout_ref, 0, d, ya_s)

    def resid_body(i, carry):
        r = _rows(i, STAT_ROWS)
        x1 = x_ref[r, :] + ya_s[r, :] + bout_ref[...]
        x1_ref[r, :] = x1
        h2_ref[r, :] = (x1 * _rsqrt_mean_sq(x1) * gffn_ref[...]).astype(BF16)
        return carry
    lax.fori_loop(0, TS // STAT_ROWS, resid_body, 0)

    lt = lax.dot_general(wr_ref[...], h2_ref[...], (((1,), (1,)), ((), ())),
                         preferred_element_type=F32) + br_ref[...]
    row8 = lax.broadcasted_iota(jnp.int32, (EXPERTS_PER_GROUP, TS), 0)
    gl = jnp.where(row8 < N_GROUPS, lt[N_EXPERTS:N_EXPERTS + 8], NEG)
    gmax = jnp.max(gl, axis=0, keepdims=True)
    gsel = jnp.min(jnp.where(gl == gmax, row8, 8), axis=0, keepdims=True)
    gw = 1.0 / jnp.sum(jnp.exp(gl - gmax), axis=0, keepdims=True)
    el = jnp.zeros((EXPERTS_PER_GROUP, TS), F32)
    for g in range(N_GROUPS):
        el = jnp.where(gsel == g, lt[g * EXPERTS_PER_GROUP:(g + 1) * EXPERTS_PER_GROUP], el)
    m1 = jnp.max(el, axis=0, keepdims=True)
    i1 = jnp.min(jnp.where(el == m1, row8, 8), axis=0, keepdims=True)
    el2 = jnp.where(row8 == i1, NEG, el)
    m2 = jnp.max(el2, axis=0, keepdims=True)
    i2 = jnp.min(jnp.where(el2 == m2, row8, 8), axis=0, keepdims=True)
    t = jnp.exp(m2 - m1)
    w0 = gw / (1.0 + t)
    w1 = gw * t / (1.0 + t)
    e0 = gsel * EXPERTS_PER_GROUP + i1
    e1 = gsel * EXPERTS_PER_GROUP + i2

    row32 = lax.broadcasted_iota(jnp.int32, (N_EXPERTS, TS), 0)
    oh0 = (row32 == e0).astype(F32)
    oh1 = (row32 == e1).astype(F32)
    c0 = jnp.sum(oh0, axis=1, keepdims=True)
    c1 = jnp.sum(oh1, axis=1, keepdims=True)
    padded = jnp.floor((c0 + c1 + (RUN_ALIGN - 1)) * (1.0 / RUN_ALIGN)) * RUN_ALIGN
    tri = (lax.broadcasted_iota(jnp.int32, (N_EXPERTS, N_EXPERTS), 0)
           > lax.broadcasted_iota(jnp.int32, (N_EXPERTS, N_EXPERTS), 1)).astype(BF16)
    units = jnp.broadcast_to(padded * (1.0 / RUN_ALIGN), (N_EXPERTS, LANES)).astype(BF16)
    lstart = RUN_ALIGN * jnp.dot(tri, units, preferred_element_type=F32)[:, 0:1]
    before = (lax.broadcasted_iota(jnp.int32, (TS, TS), 0)
              < lax.broadcasted_iota(jnp.int32, (TS, TS), 1)).astype(BF16)
    rank = jnp.dot(jnp.concatenate([oh0, oh1], axis=0).astype(BF16), before,
                   preferred_element_type=F32)
    pos0 = jnp.sum(oh0 * (lstart + rank[0:N_EXPERTS]), axis=0, keepdims=True)
    pos1 = jnp.sum(oh1 * (lstart + c0 + rank[N_EXPERTS:]), axis=0, keepdims=True)
    rt_ref[...] = jnp.concatenate(
        [pos0, pos1, w0, w1, e0.astype(F32), e1.astype(F32), jnp.zeros((2, TS), F32)], axis=0)
    cnt_ref[...] = jnp.broadcast_to(padded, (N_EXPERTS, LANES))


def _const_spec(shape):
    nd = len(shape)
    return pl.BlockSpec(shape, lambda b, s: (0,) * nd, pipeline_mode=pl.Buffered(1))


def _mixer(x, consts):
    bsz, seq, d = x.shape
    ns = seq // TS
    nt = bsz * ns
    in_specs = [pl.BlockSpec((None, TS, d), lambda b, s: (b, s, 0))]
    in_specs += [_const_spec(c.shape) for c in consts]
    out_shape = (
        jax.ShapeDtypeStruct((bsz, seq, d), F32),
        jax.ShapeDtypeStruct((bsz, seq, d), BF16),
        jax.ShapeDtypeStruct((nt, SUBLANES, TS), F32),
        jax.ShapeDtypeStruct((nt, N_EXPERTS, LANES), F32),
    )
    out_specs = (
        pl.BlockSpec((None, TS, d), lambda b, s: (b, s, 0)),
        pl.BlockSpec((None, TS, d), lambda b, s: (b, s, 0)),
        pl.BlockSpec((None, SUBLANES, TS), lambda b, s: (b * ns + s, 0, 0)),
        pl.BlockSpec((None, N_EXPERTS, LANES), lambda b, s: (b * ns + s, 0, 0)),
    )
    scratch = [
        pltpu.VMEM((TS, d), BF16),
        pltpu.VMEM((TS, 2 * d), F32),
        pltpu.VMEM((4 * d // NCOL, TS, NCOL), F32),
        pltpu.VMEM((TS + HALO, d), F32),
        pltpu.VMEM((TS, d), BF16),
        pltpu.VMEM((TS, d), F32),
        pltpu.VMEM((TS, d), F32),
        pltpu.VMEM((TS, d), F32),
        pltpu.VMEM((TS, d), BF16),
    ]
    return pl.pallas_call(
        _mixer_kernel,
        grid=(bsz, ns),
        in_specs=in_specs,
        out_specs=out_specs,
        out_shape=out_shape,
        scratch_shapes=scratch,
        compiler_params=pltpu.CompilerParams(
            dimension_semantics=("arbitrary", "arbitrary"), vmem_limit_bytes=VMEM_LIMIT),
        name="mixer",
    )(x, *consts)


def _chunk_copy(src, dst, sem, s0, d0):
    return pltpu.make_async_copy(src.at[pl.ds(s0, RUN_ALIGN)], dst.at[pl.ds(d0, RUN_ALIGN)], sem)


def _dispatch_kernel(gtab_ref, nch_ref, ztab_ref, nz_ref, nv_ref, rt_ref, h2_ref, xpad_ref, buf, zbuf, sem, zsem):
    i = pl.program_id(0)
    n = nch_ref[i]
    pos0 = rt_ref[0:1, :]
    pos1 = rt_ref[1:2, :]
    slot = lax.broadcasted_iota(jnp.int32, (RL, TS), 0).astype(F32)
    onehot = ((slot == pos0) | (slot == pos1)).astype(BF16)
    buf[...] = jnp.dot(onehot, h2_ref[...], preferred_element_type=F32)

    def issue(j, carry):
        g = pl.multiple_of(gtab_ref[i * NCH + j], RUN_ALIGN)
        _chunk_copy(buf, xpad_ref, sem, pl.multiple_of(j * RUN_ALIGN, RUN_ALIGN), g).start()
        return carry
    lax.fori_loop(0, n, issue, 0)

    @pl.when(i == pl.num_programs(0) - 1)
    def _():
        zbuf[...] = jnp.zeros((BR, D_MODEL), F32)
        nzt = nz_ref[0]
        nb = xpad_ref.shape[0] // BR

        def block_copy(b):
            return pltpu.make_async_copy(zbuf, xpad_ref.at[pl.ds(pl.multiple_of(b * BR, BR), BR)], zsem)

        def zissue(j, carry):
            g = pl.multiple_of(ztab_ref[j], RUN_ALIGN)
            _chunk_copy(zbuf, xpad_ref, zsem, 0, g).start()
            return carry
        lax.fori_loop(0, nzt, zissue, 0)

        def bissue(b, carry):
            block_copy(b).start()
            return carry
        lax.fori_loop(nv_ref[0], nb, bissue, 0)

        def zwait(j, carry):
            _chunk_copy(zbuf, xpad_ref, zsem, 0, 0).wait()
            return carry
        lax.fori_loop(0, nzt, zwait, 0)

        def bwait(b, carry):
            block_copy(0).wait()
            return carry
        lax.fori_loop(nv_ref[0], nb, bwait, 0)

    def wait(j, carry):
        _chunk_copy(buf, xpad_ref, sem, 0, 0).wait()
        return carry
    lax.fori_loop(0, n, wait, 0)


def _dispatch(gtab, nch, ztab, nz, nvalid, rt, h2, cap):
    nt = rt.shape[0]
    d = h2.shape[-1]
    grid_spec = pltpu.PrefetchScalarGridSpec(
        num_scalar_prefetch=5,
        grid=(nt,),
        in_specs=[
            pl.BlockSpec((None, SUBLANES, TS), lambda i, *_: (i, 0, 0)),
            pl.BlockSpec((TS, d), lambda i, *_: (i, 0)),
        ],
        out_specs=pl.BlockSpec(memory_space=pl.ANY),
        scratch_shapes=[
            pltpu.VMEM((RL, d), F32),
            pltpu.VMEM((BR, d), F32),
            pltpu.SemaphoreType.DMA,
            pltpu.SemaphoreType.DMA,
        ],
    )
    return pl.pallas_call(
        _dispatch_kernel,
        grid_spec=grid_spec,
        out_shape=jax.ShapeDtypeStruct((cap, d), F32),
        compiler_params=pltpu.CompilerParams(
            dimension_semantics=("arbitrary",), vmem_limit_bytes=VMEM_LIMIT),
        name="dispatch",
    )(gtab, nch, ztab, nz, nvalid, rt, h2)


def _expert_kernel(be_ref, nv_ref, x_ref, wg_ref, wu_ref, wd_ref, y_ref):
    @pl.when(pl.program_id(0) >= nv_ref[0])
    def _():
        y_ref[...] = jnp.zeros(y_ref.shape, F32)

    @pl.when(pl.program_id(0) < nv_ref[0])
    def _():
        xb = x_ref[...].astype(BF16)
        g = jnp.dot(xb, wg_ref[...], preferred_element_type=F32)
        u = jnp.dot(xb, wu_ref[...], preferred_element_type=F32)
        hid = (g * jax.nn.sigmoid(g) * u).astype(BF16)
        y_ref[...] = jnp.dot(hid, wd_ref[...], preferred_element_type=F32)


def _experts(block_e, nvalid, xpad, wg, wu, wd):
    cap, d = xpad.shape
    nb = cap // BR

    def row_map(b, be, nv):
        return (jnp.minimum(b, nv[0] - 1), 0)

    def w_map(b, be, nv):
        return (be[b], 0, 0)

    grid_spec = pltpu.PrefetchScalarGridSpec(
        num_scalar_prefetch=2,
        grid=(nb,),
        in_specs=[
            pl.BlockSpec((BR, d), row_map),
            pl.BlockSpec((None, d, D_EXPERT), w_map),
            pl.BlockSpec((None, d, D_EXPERT), w_map),
            pl.BlockSpec((None, D_EXPERT, d), w_map),
        ],
        out_specs=pl.BlockSpec((BR, d), lambda b, be, nv: (b, 0)),
    )
    return pl.pallas_call(
        _expert_kernel,
        grid_spec=grid_spec,
        out_shape=jax.ShapeDtypeStruct((cap, d), F32),
        compiler_params=pltpu.CompilerParams(
            dimension_semantics=("arbitrary",), vmem_limit_bytes=VMEM_LIMIT),
        name="experts",
    )(block_e, nvalid, xpad, wg, wu, wd)


def _combine_kernel(gtab_ref, nch_ref, rt_ref, x1_ref, gfin_ref, ypad_ref, out_ref, buf, sem):
    i = pl.program_id(0)
    n = nch_ref[i]

    def issue(j, carry):
        g = pl.multiple_of(gtab_ref[i * NCH + j], RUN_ALIGN)
        _chunk_copy(ypad_ref, buf, sem, g, pl.multiple_of(j * RUN_ALIGN, RUN_ALIGN)).start()
        return carry
    lax.fori_loop(0, n, issue, 0)

    def zero_body(j, carry):
        buf[pl.ds(pl.multiple_of(j * RUN_ALIGN, RUN_ALIGN), RUN_ALIGN), :] = jnp.zeros((RUN_ALIGN, D_MODEL), F32)
        return carry
    lax.fori_loop(n, NCH, zero_body, 0)

    pos0 = rt_ref[0:1, :]
    pos1 = rt_ref[1:2, :]
    w0 = rt_ref[2:3, :]
    w1 = rt_ref[3:4, :]
    slot = lax.broadcasted_iota(jnp.int32, (RL, TS), 0).astype(F32)
    wslot = jnp.sum(jnp.where(slot == pos0, w0, 0.0) + jnp.where(slot == pos1, w1, 0.0),
                    axis=1, keepdims=True)
    rt_cols = jnp.concatenate([rt_ref[...], jnp.zeros((LANES - SUBLANES, TS), F32)], axis=0).T
    p0c = rt_cols[:, 0:1]
    p1c = rt_cols[:, 1:2]
    lane_slot = lax.broadcasted_iota(jnp.int32, (TS, RL), 1).astype(F32)
    unsort = ((lane_slot == p0c) | (lane_slot == p1c)).astype(BF16)

    def wait(j, carry):
        _chunk_copy(ypad_ref, buf, sem, 0, 0).wait()
        return carry
    lax.fori_loop(0, n, wait, 0)

    yw = (buf[...] * wslot).astype(BF16)
    x2 = x1_ref[...] + jnp.dot(unsort, yw, preferred_element_type=F32)
    out_ref[...] = x2 * _rsqrt_mean_sq(x2) * gfin_ref[...]


def _combine(gtab, nch, rt, x1, gfin, ypad):
    nt = rt.shape[0]
    d = x1.shape[-1]
    grid_spec = pltpu.PrefetchScalarGridSpec(
        num_scalar_prefetch=2,
        grid=(nt,),
        in_specs=[
            pl.BlockSpec((None, SUBLANES, TS), lambda i, *_: (i, 0, 0)),
            pl.BlockSpec((TS, d), lambda i, *_: (i, 0)),
            pl.BlockSpec((1, d), lambda i, *_: (0, 0)),
            pl.BlockSpec(memory_space=pl.ANY),
        ],
        out_specs=pl.BlockSpec((TS, d), lambda i, *_: (i, 0)),
        scratch_shapes=[pltpu.VMEM((RL, d), F32), pltpu.SemaphoreType.DMA],
    )
    return pl.pallas_call(
        _combine_kernel,
        grid_spec=grid_spec,
        out_shape=jax.ShapeDtypeStruct((nt * TS, d), F32),
        compiler_params=pltpu.CompilerParams(
            dimension_semantics=("arbitrary",), vmem_limit_bytes=VMEM_LIMIT),
        name="combine",
    )(gtab, nch, rt, x1, gfin, ypad)


def _run_tables(cnt, cap):
    nt = cnt.shape[0]
    pc = cnt[:, :, 0].astype(jnp.int32)
    tot = jnp.sum(pc, axis=0)
    reg = ((tot + BR - 1) // BR) * BR
    reg_end = jnp.cumsum(reg)
    reg_start = reg_end - reg
    goff = reg_start[None, :] + jnp.cumsum(pc, axis=0) - pc
    lend = jnp.cumsum(pc, axis=1)
    lstart = lend - pc
    nch = (lend[:, -1] // RUN_ALIGN).astype(jnp.int32)
    local = jnp.arange(NCH, dtype=jnp.int32) * RUN_ALIGN
    in_run = (lstart[:, None, :] <= local[None, :, None]) & (local[None, :, None] < lend[:, None, :])
    gtab = local[None, :] + jnp.sum(jnp.where(in_run, (goff - lstart)[:, None, :], 0), axis=-1)
    gtab = jnp.where(jnp.arange(NCH)[None, :] < nch[:, None], gtab, 0).astype(jnp.int32).reshape(-1)
    nzmax = BR // RUN_ALIGN
    zrow = (reg_start + tot)[:, None] + jnp.arange(nzmax, dtype=jnp.int32)[None, :] * RUN_ALIGN
    zvalid = zrow < reg_end[:, None]
    zorder = jnp.argsort(~zvalid.reshape(-1), stable=True)
    ztab = jnp.where(zvalid, zrow, 0).reshape(-1)[zorder].astype(jnp.int32)
    nz = jnp.sum(zvalid).astype(jnp.int32).reshape(1)
    nb = cap // BR
    block_e = jnp.minimum(
        jnp.sum(reg_end[None, :] <= (jnp.arange(nb, dtype=jnp.int32) * BR)[:, None], axis=-1),
        N_EXPERTS - 1).astype(jnp.int32)
    nvalid = (reg_end[-1] // BR).astype(jnp.int32).reshape(1)
    return gtab, nch, ztab, nz, block_e, nvalid


def kernel(x, norm_mix_g, w_in, b_in, conv_w, conv_b, conv_ln_g, conv_ln_b, w_conv_out, b_conv_out, sg_ln_g, sg_ln_b, w_sg, b_sg, w_sg_out, b_sg_out, w_out, b_out, norm_ffn_g, w_router_group, b_router_group, w_router_expert, b_router_expert, w_expert_gate, w_expert_up, w_expert_down, norm_final_g):
    bsz, seq, d = x.shape
    assert d == D_MODEL and seq % TS == 0 and w_in.shape[0] == 1
    assert TS // CONV_ROWS == 4 * d // NCOL
    nt = bsz * seq // TS
    n_tok = bsz * seq
    cap = -(-(2 * n_tok + nt * N_EXPERTS * (RUN_ALIGN - 1) + N_EXPERTS * (BR - RUN_ALIGN)) // BR) * BR

    row = lambda v: v.reshape(1, -1).astype(F32)
    wr = jnp.zeros((ROUTER_ROWS, d), F32)
    wr = wr.at[0:N_EXPERTS].set(w_router_expert[0].T).at[N_EXPERTS:N_EXPERTS + N_GROUPS].set(w_router_group[0].T)
    br = jnp.zeros((ROUTER_ROWS, 1), F32)
    br = br.at[0:N_EXPERTS, 0].set(b_router_expert[0]).at[N_EXPERTS:N_EXPERTS + N_GROUPS, 0].set(b_router_group[0])
    consts = [
        row(norm_mix_g[0]),
        w_in[0].astype(BF16).reshape(d, 6 * d // NCOL, NCOL).transpose(1, 0, 2),
        row(b_in[0]),
        conv_w[0], row(conv_b[0]), row(conv_ln_g[0]), row(conv_ln_b[0]),
        w_conv_out[0].astype(BF16), row(b_conv_out[0]),
        row(sg_ln_g[0]), row(sg_ln_b[0]), w_sg[0], b_sg[0].T,
        w_sg_out[0].astype(BF16), row(b_sg_out[0]),
        w_out[0].astype(BF16), row(b_out[0]), row(norm_ffn_g[0]),
        wr.astype(BF16), br,
    ]
    x1, h2, rt, cnt = _mixer(x, consts)
    gtab, nch, ztab, nz, block_e, nvalid = _run_tables(cnt, cap)
    xpad = _dispatch(gtab, nch, ztab, nz, nvalid, rt, h2.reshape(n_tok, d), cap)
    ypad = _experts(block_e, nvalid, xpad, w_expert_gate[0].astype(BF16), w_expert_up[0].astype(BF16),
                    w_expert_down[0].astype(BF16))
    out = _combine(gtab, nch, rt, x1.reshape(n_tok, d), row(norm_final_g), ypad)
    return out.reshape(bsz, seq, d)
```

```python
import jax
import jax.numpy as jnp
from jax import lax
from jax.experimental import pallas as pl
from jax.experimental.pallas import tpu as pltpu

F32 = jnp.float32
BF16 = jnp.bfloat16

D_MODEL = 1024
CONV_WIDTH = 31
SG_HEADS = 8
SG_BLOCK = 128
SG_CHUNK = 64
N_GROUPS = 4
EXPERTS_PER_GROUP = 8
N_EXPERTS = N_GROUPS * EXPERTS_PER_GROUP
D_EXPERT = D_MODEL // 2
EPS = 1e-6

SUBLANES = 8
LANES = 128
TS = 512
HALO = 32
ROWS = 16
STAT_ROWS = 256
CONV_ROWS = 64
NCOL = 512
RUN_ALIGN = SUBLANES
RL = -(-(2 * TS + N_EXPERTS * (RUN_ALIGN - 1)) // LANES) * LANES
NCH = RL // RUN_ALIGN
BR = 512
ROUTER_ROWS = 64
NEG = -1e30
VMEM_LIMIT = 60 * 1024 * 1024


def _rsqrt_mean_sq(xf):
    return lax.rsqrt(jnp.mean(xf * xf, axis=-1, keepdims=True) + EPS)


def _layer_norm(xf, g, b):
    mu = jnp.mean(xf, axis=-1, keepdims=True)
    xc = xf - mu
    var = jnp.mean(xc * xc, axis=-1, keepdims=True)
    return xc * lax.rsqrt(var + EPS) * g + b


def _sigmoid(z):
    return 0.5 * jnp.tanh(0.5 * z) + 0.5


def _gelu(z):
    return 0.5 * z * (1.0 + lax.erf(z * (0.5 ** 0.5)))


def _rows(i, n):
    return pl.ds(pl.multiple_of(i * n, n), n)


def _matmul_cols(a_ref, w_ref, c0, n, out_ref):
    for j in range(0, n, NCOL):
        out_ref[:, j:j + NCOL] = jnp.dot(a_ref[...], w_ref[:, c0 + j:c0 + j + NCOL],
                                         preferred_element_type=F32)


def _mixer_kernel(x_ref, gmix_ref, win_ref, bin_ref, cw_ref, cb_ref, clg_ref, clb_ref, wco_ref, bco_ref,
                  slg_ref, slb_ref, wsg_ref, bsgt_ref, wso_ref, bso_ref, wout_ref, bout_ref, gffn_ref,
                  wr_ref, br_ref,
                  x1_ref, h2_ref, rt_ref, cnt_ref,
                  h_s, p_s, a_s, c_s, ya_s, yb_s, u_s, v_s):
    d = D_MODEL
    s_idx = pl.program_id(1)

    def norm_body(i, carry):
        r = _rows(i, STAT_ROWS)
        xf = x_ref[r, :]
        h_s[r, :] = (xf * _rsqrt_mean_sq(xf) * gmix_ref[...]).astype(BF16)
        return carry
    lax.fori_loop(0, TS // STAT_ROWS, norm_body, 0)

    nlt = d // LANES

    @pl.when(s_idx == 0)
    def _():
        a_s[:, 0:HALO, :] = jnp.zeros((nlt, HALO, LANES), F32)

    @pl.when(s_idx > 0)
    def _():
        a_s[:, 0:HALO, :] = a_s[:, TS:TS + HALO, :]

    _matmul_cols(h_s, win_ref, 0, 2 * d, p_s)

    def glu_body(i, carry):
        r = _rows(i, ROWS)
        a1 = p_s[r, 0:d] + bin_ref[:, 0:d]
        a2 = p_s[r, d:2 * d] + bin_ref[:, d:2 * d]
        a = a1 * _sigmoid(a2)
        ra = pl.ds(pl.multiple_of(i * ROWS + HALO, ROWS), ROWS)
        for c in range(nlt):
            a_s[c, ra, :] = a[:, c * LANES:(c + 1) * LANES]
        return carry
    lax.fori_loop(0, TS // ROWS, glu_body, 0)

    base_off = HALO - (CONV_WIDTH - 1)

    def conv_body(i, carry):
        r0 = pl.multiple_of(i * CONV_ROWS, CONV_ROWS)
        for c in range(nlt):
            cs = slice(c * LANES, (c + 1) * LANES)
            acc = jnp.broadcast_to(cb_ref[:, cs], (CONV_ROWS, LANES))
            for k in range(CONV_WIDTH):
                acc = acc + cw_ref[k:k + 1, cs] * a_s[c, pl.ds(r0 + (base_off + k), CONV_ROWS), :]
            p_s[pl.ds(r0, CONV_ROWS), cs] = acc
        return carry
    lax.fori_loop(0, TS // CONV_ROWS, conv_body, 0)

    def conv_ln_body(i, carry):
        r = _rows(i, STAT_ROWS)
        y = _layer_norm(p_s[r, 0:d], clg_ref[...], clb_ref[...])
        c_s[r, :] = (y * _sigmoid(y)).astype(BF16)
        return carry
    lax.fori_loop(0, TS // STAT_ROWS, conv_ln_body, 0)

    _matmul_cols(c_s, wco_ref, 0, d, ya_s)

    _matmul_cols(h_s, win_ref, 2 * d, 2 * d, p_s)

    def gelu_body(i, carry):
        r = _rows(i, STAT_ROWS)
        z1 = p_s[r, 0:d] + bin_ref[:, 2 * d:3 * d]
        z2 = p_s[r, d:2 * d] + bin_ref[:, 3 * d:4 * d]
        u_s[r, :] = _gelu(z1)
        v_s[r, :] = _layer_norm(_gelu(z2), slg_ref[...], slb_ref[...]).astype(BF16)
        return carry
    lax.fori_loop(0, TS // STAT_ROWS, gelu_body, 0)

    qi = lax.broadcasted_iota(jnp.int32, (SG_BLOCK, SG_BLOCK), 0) // SG_CHUNK
    ki = lax.broadcasted_iota(jnp.int32, (SG_BLOCK, SG_BLOCK), 1) // SG_CHUNK
    causal = qi >= ki
    for hh in range(SG_HEADS):
        wm = jnp.where(causal, wsg_ref[hh], 0.0).astype(BF16)
        cs = slice(hh * SG_BLOCK, (hh + 1) * SG_BLOCK)
        for blk in range(TS // SG_BLOCK):
            rs = slice(blk * SG_BLOCK, (blk + 1) * SG_BLOCK)
            sv = jnp.dot(wm, v_s[rs, cs], preferred_element_type=F32) + bsgt_ref[:, hh:hh + 1]
            c_s[rs, cs] = (u_s[rs, cs] * sv).astype(BF16)

    _matmul_cols(c_s, wso_ref, 0, d, yb_s)

    _matmul_cols(h_s, win_ref, 4 * d, 2 * d, p_s)

    def gate_body(i, carry):
        r = _rows(i, ROWS)
        ga = _sigmoid(p_s[r, 0:d] + bin_ref[:, 4 * d:5 * d])
        gb = _sigmoid(p_s[r, d:2 * d] + bin_ref[:, 5 * d:6 * d])
        m = ga * (ya_s[r, :] + bco_ref[...]) + gb * (yb_s[r, :] + bso_ref[...])
        c_s[r, :] = m.astype(BF16)
        return carry
    lax.fori_loop(0, TS // ROWS, gate_body, 0)

    _matmul_cols(c_s, wout_ref, 0, d, ya_s)

    def resid_body(i, carry):
        r = _rows(i, STAT_ROWS)
        x1 = x_ref[r, :] + ya_s[r, :] + bout_ref[...]
        x1_ref[r, :] = x1
        h2_ref[r, :] = (x1 * _rsqrt_mean_sq(x1) * gffn_ref[...]).astype(BF16)
        return carry
    lax.fori_loop(0, TS // STAT_ROWS, resid_body, 0)

    lt = lax.dot_general(wr_ref[...], h2_ref[...], (((1,), (1,)), ((), ())),
                         preferred_element_type=F32) + br_ref[...]
    row8 = lax.broadcasted_iota(jnp.int32, (EXPERTS_PER_GROUP, TS), 0)
    gl = jnp.where(row8 < N_GROUPS, lt[N_EXPERTS:N_EXPERTS + 8], NEG)
    gmax = jnp.max(gl, axis=0, keepdims=True)
    gsel = jnp.min(jnp.where(gl == gmax, row8, 8), axis=0, keepdims=True)
    gw = 1.0 / jnp.sum(jnp.exp(gl - gmax), axis=0, keepdims=True)
    el = jnp.zeros((EXPERTS_PER_GROUP, TS), F32)
    for g in range(N_GROUPS):
        el = jnp.where(gsel == g, lt[g * EXPERTS_PER_GROUP:(g + 1) * EXPERTS_PER_GROUP], el)
    m1 = jnp.max(el, axis=0, keepdims=True)
    i1 = jnp.min(jnp.where(el == m1, row8, 8), axis=0, keepdims=True)
    el2 = jnp.where(row8 == i1, NEG, el)
    m2 = jnp.max(el2, axis=0, keepdims=True)
    i2 = jnp.min(jnp.where(el2 == m2, row8, 8), axis=0, keepdims=True)
    t = jnp.exp(m2 - m1)
    w0 = gw / (1.0 + t)
    w1 = gw * t / (1.0 + t)
    e0 = gsel * EXPERTS_PER_GROUP + i1
    e1 = gsel * EXPERTS_PER_GROUP + i2

    row32 = lax.broadcasted_iota(jnp.int32, (N_EXPERTS, TS), 0)
    oh0 = (row32 == e0).astype(F32)
    oh1 = (row32 == e1).astype(F32)
    c0 = jnp.sum(oh0, axis=1, keepdims=True)
    c1 = jnp.sum(oh1, axis=1, keepdims=True)
    padded = jnp.floor((c0 + c1 + (RUN_ALIGN - 1)) * (1.0 / RUN_ALIGN)) * RUN_ALIGN
    tri = (lax.broadcasted_iota(jnp.int32, (N_EXPERTS, N_EXPERTS), 0)
           > lax.broadcasted_iota(jnp.int32, (N_EXPERTS, N_EXPERTS), 1)).astype(BF16)
    units = jnp.broadcast_to(padded * (1.0 / RUN_ALIGN), (N_EXPERTS, LANES)).astype(BF16)
    lstart = RUN_ALIGN * jnp.dot(tri, units, preferred_element_type=F32)[:, 0:1]
    before = (lax.broadcasted_iota(jnp.int32, (TS, TS), 0)
              < lax.broadcasted_iota(jnp.int32, (TS, TS), 1)).astype(BF16)
    rank = jnp.dot(jnp.concatenate([oh0, oh1], axis=0).astype(BF16), before,
                   preferred_element_type=F32)
    pos0 = jnp.sum(oh0 * (lstart + rank[0:N_EXPERTS]), axis=0, keepdims=True)
    pos1 = jnp.sum(oh1 * (lstart + c0 + rank[N_EXPERTS:]), axis=0, keepdims=True)
    rt_ref[...] = jnp.concatenate(
        [pos0, pos1, w0, w1, e0.astype(F32), e1.astype(F32), jnp.zeros((2, TS), F32)], axis=0)
    cnt_ref[...] = jnp.broadcast_to(padded, (N_EXPERTS, LANES))


def _const_spec(shape):
    nd = len(shape)
    return pl.BlockSpec(shape, lambda b, s: (0,) * nd, pipeline_mode=pl.Buffered(1))


def _mixer(x, consts):
    bsz, seq, d = x.shape
    ns = seq // TS
    nt = bsz * ns
    in_specs = [pl.BlockSpec((None, TS, d), lambda b, s: (b, s, 0))]
    in_specs += [_const_spec(c.shape) for c in consts]
    out_shape = (
        jax.ShapeDtypeStruct((bsz, seq, d), F32),
        jax.ShapeDtypeStruct((bsz, seq, d), BF16),
        jax.ShapeDtypeStruct((nt, SUBLANES, TS), F32),
        jax.ShapeDtypeStruct((nt, N_EXPERTS, LANES), F32),
    )
    out_specs = (
        pl.BlockSpec((None, TS, d), lambda b, s: (b, s, 0)),
        pl.BlockSpec((None, TS, d), lambda b, s: (b, s, 0)),
        pl.BlockSpec((None, SUBLANES, TS), lambda b, s: (b * ns + s, 0, 0)),
        pl.BlockSpec((None, N_EXPERTS, LANES), lambda b, s: (b * ns + s, 0, 0)),
    )
    scratch = [
        pltpu.VMEM((TS, d), BF16),
        pltpu.VMEM((TS, 2 * d), F32),
        pltpu.VMEM((d // LANES, TS + HALO, LANES), F32),
        pltpu.VMEM((TS, d), BF16),
        pltpu.VMEM((TS, d), F32),
        pltpu.VMEM((TS, d), F32),
        pltpu.VMEM((TS, d), F32),
        pltpu.VMEM((TS, d), BF16),
    ]
    return pl.pallas_call(
        _mixer_kernel,
        grid=(bsz, ns),
        in_specs=in_specs,
        out_specs=out_specs,
        out_shape=out_shape,
        scratch_shapes=scratch,
        compiler_params=pltpu.CompilerParams(
            dimension_semantics=("arbitrary", "arbitrary"), vmem_limit_bytes=VMEM_LIMIT),
        name="mixer",
    )(x, *consts)


def _chunk_copy(src, dst, sem, s0, d0):
    return pltpu.make_async_copy(src.at[pl.ds(s0, RUN_ALIGN)], dst.at[pl.ds(d0, RUN_ALIGN)], sem)


def _wait_chunks(vmem_buf, hbm_ref, sem, n, to_vmem):
    rows = pl.ds(0, n * RUN_ALIGN)
    if to_vmem:
        pltpu.make_async_copy(hbm_ref.at[rows], vmem_buf.at[rows], sem).wait()
    else:
        pltpu.make_async_copy(vmem_buf.at[rows], hbm_ref.at[rows], sem).wait()


def _dispatch_kernel(gtab_ref, nch_ref, ztab_ref, nz_ref, nv_ref, rt_ref, h2_ref, xpad_ref, buf, zbuf, sem, zsem):
    i = pl.program_id(0)
    last = pl.num_programs(0) - 1
    slot_i = i % 2
    n = nch_ref[i]
    pos0 = rt_ref[0:1, :]
    pos1 = rt_ref[1:2, :]
    slot = lax.broadcasted_iota(jnp.int32, (RL, TS), 0).astype(F32)
    onehot = ((slot == pos0) | (slot == pos1)).astype(BF16)
    buf[slot_i] = jnp.dot(onehot, h2_ref[...], preferred_element_type=F32)

    def issue(j, carry):
        g = pl.multiple_of(gtab_ref[i * NCH + j], RUN_ALIGN)
        _chunk_copy(buf.at[slot_i], xpad_ref, sem.at[slot_i], pl.multiple_of(j * RUN_ALIGN, RUN_ALIGN), g).start()
        return carry
    lax.fori_loop(0, n, issue, 0)

    @pl.when(i > 0)
    def _():
        _wait_chunks(buf.at[1 - slot_i], xpad_ref, sem.at[1 - slot_i], nch_ref[i - 1], to_vmem=False)

    @pl.when(i == last)
    def _():
        zbuf[...] = jnp.zeros((BR, D_MODEL), F32)
        nzt = nz_ref[0]
        nb = xpad_ref.shape[0] // BR

        def block_copy(b):
            return pltpu.make_async_copy(zbuf, xpad_ref.at[pl.ds(pl.multiple_of(b * BR, BR), BR)], zsem)

        def zissue(j, carry):
            g = pl.multiple_of(ztab_ref[j], RUN_ALIGN)
            _chunk_copy(zbuf, xpad_ref, zsem, 0, g).start()
            return carry
        lax.fori_loop(0, nzt, zissue, 0)

        def bissue(b, carry):
            block_copy(b).start()
            return carry
        lax.fori_loop(nv_ref[0], nb, bissue, 0)

        def zwait(j, carry):
            _chunk_copy(zbuf, xpad_ref, zsem, 0, 0).wait()
            return carry
        lax.fori_loop(0, nzt, zwait, 0)

        def bwait(b, carry):
            block_copy(0).wait()
            return carry
        lax.fori_loop(nv_ref[0], nb, bwait, 0)

        _wait_chunks(buf.at[slot_i], xpad_ref, sem.at[slot_i], n, to_vmem=False)


def _dispatch(gtab, nch, ztab, nz, nvalid, rt, h2, cap):
    nt = rt.shape[0]
    d = h2.shape[-1]
    grid_spec = pltpu.PrefetchScalarGridSpec(
        num_scalar_prefetch=5,
        grid=(nt,),
        in_specs=[
            pl.BlockSpec((None, SUBLANES, TS), lambda i, *_: (i, 0, 0)),
            pl.BlockSpec((TS, d), lambda i, *_: (i, 0)),
        ],
        out_specs=pl.BlockSpec(memory_space=pl.ANY),
        scratch_shapes=[
            pltpu.VMEM((2, RL, d), F32),
            pltpu.VMEM((BR, d), F32),
            pltpu.SemaphoreType.DMA((2,)),
            pltpu.SemaphoreType.DMA,
        ],
    )
    return pl.pallas_call(
        _dispatch_kernel,
        grid_spec=grid_spec,
        out_shape=jax.ShapeDtypeStruct((cap, d), F32),
        compiler_params=pltpu.CompilerParams(
            dimension_semantics=("arbitrary",), vmem_limit_bytes=VMEM_LIMIT),
        name="dispatch",
    )(gtab, nch, ztab, nz, nvalid, rt, h2)


def _expert_kernel(be_ref, first_ref, nv_ref, x_ref, wg_ref, wu_ref, wd_ref, y_ref, wg_s, wu_s, wd_s):
    b = pl.program_id(0)

    @pl.when(b >= nv_ref[0])
    def _():
        y_ref[...] = jnp.zeros(y_ref.shape, F32)

    @pl.when((b < nv_ref[0]) & (first_ref[b] == 1))
    def _():
        wg_s[...] = wg_ref[...].astype(BF16)
        wu_s[...] = wu_ref[...].astype(BF16)
        wd_s[...] = wd_ref[...].astype(BF16)

    @pl.when(b < nv_ref[0])
    def _():
        xb = x_ref[...].astype(BF16)
        g = jnp.dot(xb, wg_s[...], preferred_element_type=F32)
        u = jnp.dot(xb, wu_s[...], preferred_element_type=F32)
        hid = (g * _sigmoid(g) * u).astype(BF16)
        y_ref[...] = jnp.dot(hid, wd_s[...], preferred_element_type=F32)


def _experts(block_e, first, nvalid, xpad, wg, wu, wd):
    cap, d = xpad.shape
    nb = cap // BR

    def row_map(b, be, fi, nv):
        return (jnp.minimum(b, nv[0] - 1), 0)

    def w_map(b, be, fi, nv):
        return (be[b], 0, 0)

    grid_spec = pltpu.PrefetchScalarGridSpec(
        num_scalar_prefetch=3,
        grid=(nb,),
        in_specs=[
            pl.BlockSpec((BR, d), row_map),
            pl.BlockSpec((None, d, D_EXPERT), w_map),
            pl.BlockSpec((None, d, D_EXPERT), w_map),
            pl.BlockSpec((None, D_EXPERT, d), w_map),
        ],
        out_specs=pl.BlockSpec((BR, d), lambda b, be, fi, nv: (b, 0)),
        scratch_shapes=[
            pltpu.VMEM((d, D_EXPERT), BF16),
            pltpu.VMEM((d, D_EXPERT), BF16),
            pltpu.VMEM((D_EXPERT, d), BF16),
        ],
    )
    return pl.pallas_call(
        _expert_kernel,
        grid_spec=grid_spec,
        out_shape=jax.ShapeDtypeStruct((cap, d), F32),
        compiler_params=pltpu.CompilerParams(
            dimension_semantics=("arbitrary",), vmem_limit_bytes=VMEM_LIMIT),
        name="experts",
    )(block_e, first, nvalid, xpad, wg, wu, wd)


def _combine_kernel(gtab_ref, nch_ref, rt_ref, x1_ref, gfin_ref, ypad_ref, out_ref, buf, sem):
    i = pl.program_id(0)
    last = pl.num_programs(0) - 1
    slot_i = i % 2

    def fetch(t, slot_t):
        nt_ = nch_ref[t]

        def issue(j, carry):
            g = pl.multiple_of(gtab_ref[t * NCH + j], RUN_ALIGN)
            _chunk_copy(ypad_ref, buf.at[slot_t], sem.at[slot_t], g, pl.multiple_of(j * RUN_ALIGN, RUN_ALIGN)).start()
            return carry
        lax.fori_loop(0, nt_, issue, 0)

        def zero_body(j, carry):
            buf[slot_t, pl.ds(pl.multiple_of(j * RUN_ALIGN, RUN_ALIGN), RUN_ALIGN), :] = jnp.zeros(
                (RUN_ALIGN, D_MODEL), F32)
            return carry
        lax.fori_loop(nt_, NCH, zero_body, 0)

    @pl.when(i == 0)
    def _():
        fetch(0, 0)

    @pl.when(i < last)
    def _():
        fetch(i + 1, 1 - slot_i)

    rt_cols = jnp.concatenate([rt_ref[...], jnp.zeros((LANES - SUBLANES, TS), F32)], axis=0).T
    p0c = rt_cols[:, 0:1]
    p1c = rt_cols[:, 1:2]
    w0c = rt_cols[:, 2:3]
    w1c = rt_cols[:, 3:4]
    lane_slot = lax.broadcasted_iota(jnp.int32, (TS, RL), 1).astype(F32)
    unsort = jnp.where(lane_slot == p0c, w0c, jnp.where(lane_slot == p1c, w1c, 0.0)).astype(BF16)

    _wait_chunks(buf.at[slot_i], ypad_ref, sem.at[slot_i], nch_ref[i], to_vmem=True)

    x2 = x1_ref[...] + jnp.dot(unsort, buf[slot_i].astype(BF16), preferred_element_type=F32)
    out_ref[...] = x2 * _rsqrt_mean_sq(x2) * gfin_ref[...]


def _combine(gtab, nch, rt, x1, gfin, ypad):
    nt = rt.shape[0]
    d = x1.shape[-1]
    grid_spec = pltpu.PrefetchScalarGridSpec(
        num_scalar_prefetch=2,
        grid=(nt,),
        in_specs=[
            pl.BlockSpec((None, SUBLANES, TS), lambda i, *_: (i, 0, 0)),
            pl.BlockSpec((TS, d), lambda i, *_: (i, 0)),
            pl.BlockSpec((1, d), lambda i, *_: (0, 0)),
            pl.BlockSpec(memory_space=pl.ANY),
        ],
        out_specs=pl.BlockSpec((TS, d), lambda i, *_: (i, 0)),
        scratch_shapes=[pltpu.VMEM((2, RL, d), F32), pltpu.SemaphoreType.DMA((2,))],
    )
    return pl.pallas_call(
        _combine_kernel,
        grid_spec=grid_spec,
        out_shape=jax.ShapeDtypeStruct((nt * TS, d), F32),
        compiler_params=pltpu.CompilerParams(
            dimension_semantics=("arbitrary",), vmem_limit_bytes=VMEM_LIMIT),
        name="combine",
    )(gtab, nch, rt, x1, gfin, ypad)


def _run_tables(cnt, cap):
    pc = cnt[:, :, 0].astype(jnp.int32)
    tot = jnp.sum(pc, axis=0)
    reg = ((tot + BR - 1) // BR) * BR
    reg_end = jnp.cumsum(reg)
    reg_start = reg_end - reg
    goff = reg_start[None, :] + jnp.cumsum(pc, axis=0) - pc
    lend = jnp.cumsum(pc, axis=1)
    lstart = lend - pc
    nch = (lend[:, -1] // RUN_ALIGN).astype(jnp.int32)
    local = jnp.arange(NCH, dtype=jnp.int32) * RUN_ALIGN
    in_run = (lstart[:, None, :] <= local[None, :, None]) & (local[None, :, None] < lend[:, None, :])
    gtab = local[None, :] + jnp.sum(jnp.where(in_run, (goff - lstart)[:, None, :], 0), axis=-1)
    gtab = jnp.where(jnp.arange(NCH)[None, :] < nch[:, None], gtab, 0).astype(jnp.int32).reshape(-1)
    nzmax = BR // RUN_ALIGN
    zrow = (reg_start + tot)[:, None] + jnp.arange(nzmax, dtype=jnp.int32)[None, :] * RUN_ALIGN
    zvalid = zrow < reg_end[:, None]
    zorder = jnp.argsort(~zvalid.reshape(-1), stable=True)
    ztab = jnp.where(zvalid, zrow, 0).reshape(-1)[zorder].astype(jnp.int32)
    nz = jnp.sum(zvalid).astype(jnp.int32).reshape(1)
    nb = cap // BR
    block_e = jnp.minimum(
        jnp.sum(reg_end[None, :] <= (jnp.arange(nb, dtype=jnp.int32) * BR)[:, None], axis=-1),
        N_EXPERTS - 1).astype(jnp.int32)
    first = jnp.concatenate([jnp.ones((1,), jnp.int32), (block_e[1:] != block_e[:-1]).astype(jnp.int32)])
    nvalid = (reg_end[-1] // BR).astype(jnp.int32).reshape(1)
    return gtab, nch, ztab, nz, block_e, first, nvalid


def kernel(x, norm_mix_g, w_in, b_in, conv_w, conv_b, conv_ln_g, conv_ln_b, w_conv_out, b_conv_out, sg_ln_g, sg_ln_b, w_sg, b_sg, w_sg_out, b_sg_out, w_out, b_out, norm_ffn_g, w_router_group, b_router_group, w_router_expert, b_router_expert, w_expert_gate, w_expert_up, w_expert_down, norm_final_g):
    bsz, seq, d = x.shape
    assert d == D_MODEL and seq % TS == 0 and w_in.shape[0] == 1
    nt = bsz * seq // TS
    n_tok = bsz * seq
    cap = -(-(2 * n_tok + nt * N_EXPERTS * (RUN_ALIGN - 1) + N_EXPERTS * (BR - RUN_ALIGN)) // BR) * BR

    row = lambda v: v.reshape(1, -1).astype(F32)
    wr = jnp.zeros((ROUTER_ROWS, d), F32)
    wr = wr.at[0:N_EXPERTS].set(w_router_expert[0].T).at[N_EXPERTS:N_EXPERTS + N_GROUPS].set(w_router_group[0].T)
    br = jnp.zeros((ROUTER_ROWS, 1), F32)
    br = br.at[0:N_EXPERTS, 0].set(b_router_expert[0]).at[N_EXPERTS:N_EXPERTS + N_GROUPS, 0].set(b_router_group[0])
    consts = [
        row(norm_mix_g[0]), w_in[0].astype(BF16), row(b_in[0]),
        conv_w[0], row(conv_b[0]), row(conv_ln_g[0]), row(conv_ln_b[0]),
        w_conv_out[0].astype(BF16), row(b_conv_out[0]),
        row(sg_ln_g[0]), row(sg_ln_b[0]), w_sg[0], b_sg[0].T,
        w_sg_out[0].astype(BF16), row(b_sg_out[0]),
        w_out[0].astype(BF16), row(b_out[0]), row(norm_ffn_g[0]),
        wr.astype(BF16), br,
    ]
    x1, h2, rt, cnt = _mixer(x, consts)
    gtab, nch, ztab, nz, block_e, first, nvalid = _run_tables(cnt, cap)
    xpad = _dispatch(gtab, nch, ztab, nz, nvalid, rt, h2.reshape(n_tok, d), cap)
    ypad = _experts(block_e, first, nvalid, xpad, w_expert_gate[0], w_expert_up[0], w_expert_down[0])
    out = _combine(gtab, nch, rt, x1.reshape(n_tok, d), row(norm_final_g), ypad)
    return out.reshape(bsz, seq, d)
```

```python
import jax
import jax.numpy as jnp
from jax import lax
from jax.experimental import pallas as pl
from jax.experimental.pallas import tpu as pltpu

F32 = jnp.float32
BF16 = jnp.bfloat16

D_MODEL = 1024
CONV_WIDTH = 31
SG_HEADS = 8
SG_BLOCK = 128
SG_CHUNK = 64
N_GROUPS = 4
EXPERTS_PER_GROUP = 8
N_EXPERTS = N_GROUPS * EXPERTS_PER_GROUP
D_EXPERT = D_MODEL // 2
EPS = 1e-6

SUBLANES = 8
LANES = 128
TS = 512
HALO = 32
ROWS = 16
STAT_ROWS = 256
CONV_ROWS = 64
NCOL = 512
RUN_ALIGN = SUBLANES
RL = -(-(2 * TS + N_EXPERTS * (RUN_ALIGN - 1)) // LANES) * LANES
NCH = RL // RUN_ALIGN
BIG = 32
NBIG = RL // BIG
NSMALL = N_EXPERTS * (BIG // RUN_ALIGN - 1)
LOCAL_BITS = 11
BR = 1024
ZBIG = 64
ROUTER_ROWS = 64
NEG = -1e30
assert RL <= 1 << LOCAL_BITS
VMEM_LIMIT = 60 * 1024 * 1024


def _rsqrt_mean_sq(xf):
    return lax.rsqrt(jnp.mean(xf * xf, axis=-1, keepdims=True) + EPS)


def _layer_norm(xf, g, b):
    mu = jnp.mean(xf, axis=-1, keepdims=True)
    xc = xf - mu
    var = jnp.mean(xc * xc, axis=-1, keepdims=True)
    return xc * lax.rsqrt(var + EPS) * g + b


def _sigmoid(z):
    return 0.5 * jnp.tanh(0.5 * z) + 0.5


def _gelu(z):
    return 0.5 * z * (1.0 + lax.erf(z * (0.5 ** 0.5)))


def _rows(i, n):
    return pl.ds(pl.multiple_of(i * n, n), n)


def _matmul_cols(a_ref, w_ref, c0, n, out_ref):
    for j in range(0, n, NCOL):
        out_ref[:, j:j + NCOL] = jnp.dot(a_ref[...], w_ref[:, c0 + j:c0 + j + NCOL],
                                         preferred_element_type=F32)


def _mixer_kernel(x_ref, gmix_ref, win_ref, bin_ref, cw_ref, cb_ref, clg_ref, clb_ref, wco_ref, bco_ref,
                  slg_ref, slb_ref, wsg_ref, bsgt_ref, wso_ref, bso_ref, wout_ref, bout_ref, gffn_ref,
                  wr_ref, br_ref, before_ref,
                  x1_ref, h2_ref, rt_ref, cnt_ref,
                  h_s, p_s, a_s, c_s, ya_s, yb_s, u_s, v_s):
    d = D_MODEL
    s_idx = pl.program_id(1)

    def norm_body(i, carry):
        r = _rows(i, STAT_ROWS)
        xf = x_ref[r, :]
        h_s[r, :] = (xf * _rsqrt_mean_sq(xf) * gmix_ref[...]).astype(BF16)
        return carry
    lax.fori_loop(0, TS // STAT_ROWS, norm_body, 0)

    nlt = d // LANES

    @pl.when(s_idx == 0)
    def _():
        a_s[:, 0:HALO, :] = jnp.zeros((nlt, HALO, LANES), F32)

    @pl.when(s_idx > 0)
    def _():
        a_s[:, 0:HALO, :] = a_s[:, TS:TS + HALO, :]

    _matmul_cols(h_s, win_ref, 0, 2 * d, p_s)

    def glu_body(i, carry):
        r = _rows(i, ROWS)
        a1 = p_s[r, 0:d] + bin_ref[:, 0:d]
        a2 = p_s[r, d:2 * d] + bin_ref[:, d:2 * d]
        a = a1 * _sigmoid(a2)
        ra = pl.ds(pl.multiple_of(i * ROWS + HALO, ROWS), ROWS)
        for c in range(nlt):
            a_s[c, ra, :] = a[:, c * LANES:(c + 1) * LANES]
        return carry
    lax.fori_loop(0, TS // ROWS, glu_body, 0)

    base_off = HALO - (CONV_WIDTH - 1)

    def conv_body(i, carry):
        r0 = pl.multiple_of(i * CONV_ROWS, CONV_ROWS)
        for c in range(nlt):
            cs = slice(c * LANES, (c + 1) * LANES)
            acc = jnp.broadcast_to(cb_ref[:, cs], (CONV_ROWS, LANES))
            for k in range(CONV_WIDTH):
                acc = acc + cw_ref[k:k + 1, cs] * a_s[c, pl.ds(r0 + (base_off + k), CONV_ROWS), :]
            p_s[pl.ds(r0, CONV_ROWS), cs] = acc
        return carry
    lax.fori_loop(0, TS // CONV_ROWS, conv_body, 0)

    def conv_ln_body(i, carry):
        r = _rows(i, STAT_ROWS)
        y = _layer_norm(p_s[r, 0:d], clg_ref[...], clb_ref[...])
        c_s[r, :] = (y * _sigmoid(y)).astype(BF16)
        return carry
    lax.fori_loop(0, TS // STAT_ROWS, conv_ln_body, 0)

    _matmul_cols(c_s, wco_ref, 0, d, ya_s)

    _matmul_cols(h_s, win_ref, 2 * d, 2 * d, p_s)

    def gelu_body(i, carry):
        r = _rows(i, STAT_ROWS)
        z1 = p_s[r, 0:d] + bin_ref[:, 2 * d:3 * d]
        z2 = p_s[r, d:2 * d] + bin_ref[:, 3 * d:4 * d]
        u_s[r, :] = _gelu(z1)
        v_s[r, :] = _layer_norm(_gelu(z2), slg_ref[...], slb_ref[...]).astype(BF16)
        return carry
    lax.fori_loop(0, TS // STAT_ROWS, gelu_body, 0)

    qi = lax.broadcasted_iota(jnp.int32, (SG_BLOCK, SG_BLOCK), 0) // SG_CHUNK
    ki = lax.broadcasted_iota(jnp.int32, (SG_BLOCK, SG_BLOCK), 1) // SG_CHUNK
    causal = qi >= ki
    for hh in range(SG_HEADS):
        wm = jnp.where(causal, wsg_ref[hh], 0.0).astype(BF16)
        cs = slice(hh * SG_BLOCK, (hh + 1) * SG_BLOCK)
        for blk in range(TS // SG_BLOCK):
            rs = slice(blk * SG_BLOCK, (blk + 1) * SG_BLOCK)
            sv = jnp.dot(wm, v_s[rs, cs], preferred_element_type=F32) + bsgt_ref[:, hh:hh + 1]
            c_s[rs, cs] = (u_s[rs, cs] * sv).astype(BF16)

    _matmul_cols(c_s, wso_ref, 0, d, yb_s)

    _matmul_cols(h_s, win_ref, 4 * d, 2 * d, p_s)

    def gate_body(i, carry):
        r = _rows(i, ROWS)
        ga = _sigmoid(p_s[r, 0:d] + bin_ref[:, 4 * d:5 * d])
        gb = _sigmoid(p_s[r, d:2 * d] + bin_ref[:, 5 * d:6 * d])
        m = ga * (ya_s[r, :] + bco_ref[...]) + gb * (yb_s[r, :] + bso_ref[...])
        c_s[r, :] = m.astype(BF16)
        return carry
    lax.fori_loop(0, TS // ROWS, gate_body, 0)

    _matmul_cols(c_s, wout_ref, 0, d, ya_s)

    def resid_body(i, carry):
        r = _rows(i, STAT_ROWS)
        x1 = x_ref[r, :] + ya_s[r, :] + bout_ref[...]
        x1_ref[r, :] = x1
        h2_ref[r, :] = (x1 * _rsqrt_mean_sq(x1) * gffn_ref[...]).astype(BF16)
        return carry
    lax.fori_loop(0, TS // STAT_ROWS, resid_body, 0)

    lt = lax.dot_general(wr_ref[...], h2_ref[...], (((1,), (1,)), ((), ())),
                         preferred_element_type=F32) + br_ref[...]
    row8 = lax.broadcasted_iota(jnp.int32, (EXPERTS_PER_GROUP, TS), 0)
    gl = jnp.where(row8 < N_GROUPS, lt[N_EXPERTS:N_EXPERTS + 8], NEG)
    gmax = jnp.max(gl, axis=0, keepdims=True)
    gsel = jnp.min(jnp.where(gl == gmax, row8, 8), axis=0, keepdims=True)
    gw = 1.0 / jnp.sum(jnp.exp(gl - gmax), axis=0, keepdims=True)
    el = jnp.zeros((EXPERTS_PER_GROUP, TS), F32)
    for g in range(N_GROUPS):
        el = jnp.where(gsel == g, lt[g * EXPERTS_PER_GROUP:(g + 1) * EXPERTS_PER_GROUP], el)
    m1 = jnp.max(el, axis=0, keepdims=True)
    i1 = jnp.min(jnp.where(el == m1, row8, 8), axis=0, keepdims=True)
    el2 = jnp.where(row8 == i1, NEG, el)
    m2 = jnp.max(el2, axis=0, keepdims=True)
    i2 = jnp.min(jnp.where(el2 == m2, row8, 8), axis=0, keepdims=True)
    t = jnp.exp(m2 - m1)
    w0 = gw / (1.0 + t)
    w1 = gw * t / (1.0 + t)
    e0 = gsel * EXPERTS_PER_GROUP + i1
    e1 = gsel * EXPERTS_PER_GROUP + i2

    row32 = lax.broadcasted_iota(jnp.int32, (N_EXPERTS, TS), 0)
    oh0 = (row32 == e0).astype(F32)
    oh1 = (row32 == e1).astype(F32)
    c0 = jnp.sum(oh0, axis=1, keepdims=True)
    c1 = jnp.sum(oh1, axis=1, keepdims=True)
    padded = jnp.floor((c0 + c1 + (RUN_ALIGN - 1)) * (1.0 / RUN_ALIGN)) * RUN_ALIGN
    tri = (lax.broadcasted_iota(jnp.int32, (N_EXPERTS, N_EXPERTS), 0)
           > lax.broadcasted_iota(jnp.int32, (N_EXPERTS, N_EXPERTS), 1)).astype(BF16)
    units = jnp.broadcast_to(padded * (1.0 / RUN_ALIGN), (N_EXPERTS, LANES)).astype(BF16)
    lstart = RUN_ALIGN * jnp.dot(tri, units, preferred_element_type=F32)[:, 0:1]
    rank = jnp.dot(jnp.concatenate([oh0, oh1], axis=0).astype(BF16), before_ref[...],
                   preferred_element_type=F32)
    pos0 = jnp.sum(oh0 * (lstart + rank[0:N_EXPERTS]), axis=0, keepdims=True)
    pos1 = jnp.sum(oh1 * (lstart + c0 + rank[N_EXPERTS:]), axis=0, keepdims=True)
    rt_ref[...] = jnp.concatenate(
        [pos0, pos1, w0, w1, e0.astype(F32), e1.astype(F32), jnp.zeros((2, TS), F32)], axis=0)
    cnt_ref[...] = jnp.broadcast_to(padded, (N_EXPERTS, LANES))


def _const_spec(shape):
    nd = len(shape)
    return pl.BlockSpec(shape, lambda b, s: (0,) * nd, pipeline_mode=pl.Buffered(1))


def _mixer(x, consts):
    bsz, seq, d = x.shape
    ns = seq // TS
    nt = bsz * ns
    in_specs = [pl.BlockSpec((None, TS, d), lambda b, s: (b, s, 0))]
    in_specs += [_const_spec(c.shape) for c in consts]
    out_shape = (
        jax.ShapeDtypeStruct((bsz, seq, d), F32),
        jax.ShapeDtypeStruct((bsz, seq, d), BF16),
        jax.ShapeDtypeStruct((nt, SUBLANES, TS), F32),
        jax.ShapeDtypeStruct((nt, N_EXPERTS, LANES), F32),
    )
    out_specs = (
        pl.BlockSpec((None, TS, d), lambda b, s: (b, s, 0)),
        pl.BlockSpec((None, TS, d), lambda b, s: (b, s, 0)),
        pl.BlockSpec((None, SUBLANES, TS), lambda b, s: (b * ns + s, 0, 0)),
        pl.BlockSpec((None, N_EXPERTS, LANES), lambda b, s: (b * ns + s, 0, 0)),
    )
    scratch = [
        pltpu.VMEM((TS, d), BF16),
        pltpu.VMEM((TS, 2 * d), F32),
        pltpu.VMEM((d // LANES, TS + HALO, LANES), F32),
        pltpu.VMEM((TS, d), BF16),
        pltpu.VMEM((TS, d), F32),
        pltpu.VMEM((TS, d), F32),
        pltpu.VMEM((TS, d), F32),
        pltpu.VMEM((TS, d), BF16),
    ]
    return pl.pallas_call(
        _mixer_kernel,
        grid=(bsz, ns),
        in_specs=in_specs,
        out_specs=out_specs,
        out_shape=out_shape,
        scratch_shapes=scratch,
        compiler_params=pltpu.CompilerParams(
            dimension_semantics=("arbitrary", "arbitrary"), vmem_limit_bytes=VMEM_LIMIT),
        name="mixer",
    )(x, *consts)


def _run_copy(src, dst, sem, s0, d0, rows):
    return pltpu.make_async_copy(src.at[pl.ds(pl.multiple_of(s0, RUN_ALIGN), rows)],
                                 dst.at[pl.ds(pl.multiple_of(d0, RUN_ALIGN), rows)], sem)


def _start_run_copies(big_ref, small_ref, nbig, nsmall, t, vmem_buf, hbm_ref, sem, to_vmem):
    def start(entry, rows):
        local = entry & ((1 << LOCAL_BITS) - 1)
        glob = lax.shift_right_logical(entry, LOCAL_BITS)
        if to_vmem:
            _run_copy(hbm_ref, vmem_buf, sem, glob, local, rows).start()
        else:
            _run_copy(vmem_buf, hbm_ref, sem, local, glob, rows).start()

    def big_body(j, carry):
        start(big_ref[t * NBIG + j], BIG)
        return carry
    lax.fori_loop(0, nbig, big_body, 0)

    def small_body(j, carry):
        start(small_ref[t * NSMALL + j], RUN_ALIGN)
        return carry
    lax.fori_loop(0, nsmall, small_body, 0)


def _wait_run_copies(vmem_buf, hbm_ref, sem, rows, to_vmem):
    sl = pl.ds(0, pl.multiple_of(rows, RUN_ALIGN))
    if to_vmem:
        pltpu.make_async_copy(hbm_ref.at[sl], vmem_buf.at[sl], sem).wait()
    else:
        pltpu.make_async_copy(vmem_buf.at[sl], hbm_ref.at[sl], sem).wait()


def _dispatch_kernel(big_ref, small_ref, cnts_ref, zbig_ref, zsmall_ref, zc_ref, rt_ref, h2_ref, xpad_ref,
                     buf, zbuf, sem, zsem):
    i = pl.program_id(0)
    nt = pl.num_programs(0)
    slot_i = i % 2
    pos0 = rt_ref[0:1, :]
    pos1 = rt_ref[1:2, :]
    slot = lax.broadcasted_iota(jnp.int32, (RL, TS), 0).astype(F32)
    onehot = ((slot == pos0) | (slot == pos1)).astype(BF16)
    buf[slot_i] = jnp.dot(onehot, h2_ref[...], preferred_element_type=F32)
    _start_run_copies(big_ref, small_ref, cnts_ref[i], cnts_ref[nt + i], i, buf.at[slot_i], xpad_ref,
                      sem.at[slot_i], to_vmem=False)

    @pl.when(i > 0)
    def _():
        _wait_run_copies(buf.at[1 - slot_i], xpad_ref, sem.at[1 - slot_i], cnts_ref[2 * nt + i - 1], to_vmem=False)

    @pl.when(i == nt - 1)
    def _():
        zbuf[...] = jnp.zeros((BR, D_MODEL), F32)
        nb = xpad_ref.shape[0] // BR

        def zcopy(row, rows):
            return pltpu.make_async_copy(zbuf.at[pl.ds(0, rows)],
                                         xpad_ref.at[pl.ds(pl.multiple_of(row, RUN_ALIGN), rows)], zsem)

        def fill(tab_ref, count, rows):
            def issue(j, carry):
                zcopy(tab_ref[j], rows).start()
                return carry
            lax.fori_loop(0, count, issue, 0)

            def wait(j, carry):
                zcopy(0, rows).wait()
                return carry
            lax.fori_loop(0, count, wait, 0)

        fill(zsmall_ref, zc_ref[0], RUN_ALIGN)
        fill(zbig_ref, zc_ref[1], ZBIG)

        def bissue(b, carry):
            zcopy(b * BR, BR).start()
            return carry
        lax.fori_loop(zc_ref[2], nb, bissue, 0)

        def bwait(b, carry):
            zcopy(0, BR).wait()
            return carry
        lax.fori_loop(zc_ref[2], nb, bwait, 0)

        _wait_run_copies(buf.at[slot_i], xpad_ref, sem.at[slot_i], cnts_ref[2 * nt + i], to_vmem=False)


def _dispatch(tabs, rt, h2, cap):
    nt = rt.shape[0]
    d = h2.shape[-1]
    grid_spec = pltpu.PrefetchScalarGridSpec(
        num_scalar_prefetch=6,
        grid=(nt,),
        in_specs=[
            pl.BlockSpec((None, SUBLANES, TS), lambda i, *_: (i, 0, 0)),
            pl.BlockSpec((TS, d), lambda i, *_: (i, 0)),
        ],
        out_specs=pl.BlockSpec(memory_space=pl.ANY),
        scratch_shapes=[
            pltpu.VMEM((2, RL, d), F32),
            pltpu.VMEM((BR, d), F32),
            pltpu.SemaphoreType.DMA((2,)),
            pltpu.SemaphoreType.DMA,
        ],
    )
    return pl.pallas_call(
        _dispatch_kernel,
        grid_spec=grid_spec,
        out_shape=jax.ShapeDtypeStruct((cap, d), F32),
        compiler_params=pltpu.CompilerParams(
            dimension_semantics=("arbitrary",), vmem_limit_bytes=VMEM_LIMIT),
        name="dispatch",
    )(tabs["big"], tabs["small"], tabs["cnts"], tabs["zbig"], tabs["zsmall"], tabs["zc"], rt, h2)


def _expert_kernel(be_ref, first_ref, wslot_ref, nexte_ref, nv_ref, x_ref, wg_hbm, wu_hbm, wd_hbm, y_ref,
                   wg_f, wu_f, wd_f, wg_s, wu_s, wd_s, sem):
    b = pl.program_id(0)

    def weight_copies(e, ws):
        return (pltpu.make_async_copy(wg_hbm.at[e], wg_f.at[ws], sem.at[0, ws]),
                pltpu.make_async_copy(wu_hbm.at[e], wu_f.at[ws], sem.at[1, ws]),
                pltpu.make_async_copy(wd_hbm.at[e], wd_f.at[ws], sem.at[2, ws]))

    @pl.when(b >= nv_ref[0])
    def _():
        y_ref[...] = jnp.zeros(y_ref.shape, F32)

    @pl.when((b < nv_ref[0]) & (first_ref[b] == 1))
    def _():
        e = be_ref[b]
        ws = wslot_ref[b]
        ne = nexte_ref[b]

        @pl.when(b == 0)
        def _():
            for c in weight_copies(e, ws):
                c.start()

        @pl.when(ne >= 0)
        def _():
            for c in weight_copies(ne, 1 - ws):
                c.start()

        for c in weight_copies(e, ws):
            c.wait()
        wg_s[...] = wg_f[ws].astype(BF16)
        wu_s[...] = wu_f[ws].astype(BF16)
        wd_s[...] = wd_f[ws].astype(BF16)

    @pl.when(b < nv_ref[0])
    def _():
        xb = x_ref[...].astype(BF16)
        g = jnp.dot(xb, wg_s[...], preferred_element_type=F32)
        u = jnp.dot(xb, wu_s[...], preferred_element_type=F32)
        hid = (g * _sigmoid(g) * u).astype(BF16)
        y_ref[...] = jnp.dot(hid, wd_s[...], preferred_element_type=F32)


def _experts(tabs, xpad, wg, wu, wd):
    cap, d = xpad.shape
    nb = cap // BR

    def row_map(b, be, fi, ws, ne, nv):
        return (jnp.minimum(b, nv[0] - 1), 0)

    grid_spec = pltpu.PrefetchScalarGridSpec(
        num_scalar_prefetch=5,
        grid=(nb,),
        in_specs=[
            pl.BlockSpec((BR, d), row_map),
            pl.BlockSpec(memory_space=pl.ANY),
            pl.BlockSpec(memory_space=pl.ANY),
            pl.BlockSpec(memory_space=pl.ANY),
        ],
        out_specs=pl.BlockSpec((BR, d), lambda b, *_: (b, 0)),
        scratch_shapes=[
            pltpu.VMEM((2, d, D_EXPERT), F32),
            pltpu.VMEM((2, d, D_EXPERT), F32),
            pltpu.VMEM((2, D_EXPERT, d), F32),
            pltpu.VMEM((d, D_EXPERT), BF16),
            pltpu.VMEM((d, D_EXPERT), BF16),
            pltpu.VMEM((D_EXPERT, d), BF16),
            pltpu.SemaphoreType.DMA((3, 2)),
        ],
    )
    return pl.pallas_call(
        _expert_kernel,
        grid_spec=grid_spec,
        out_shape=jax.ShapeDtypeStruct((cap, d), F32),
        compiler_params=pltpu.CompilerParams(
            dimension_semantics=("arbitrary",), vmem_limit_bytes=VMEM_LIMIT),
        name="experts",
    )(tabs["block_e"], tabs["first"], tabs["wslot"], tabs["next_e"], tabs["nvalid"], xpad, wg, wu, wd)


def _combine_kernel(big_ref, small_ref, cnts_ref, rt_ref, x1_ref, gfin_ref, ypad_ref, out_ref, buf, sem):
    i = pl.program_id(0)
    nt = pl.num_programs(0)
    slot_i = i % 2

    def fetch(t, slot_t):
        _start_run_copies(big_ref, small_ref, cnts_ref[t], cnts_ref[nt + t], t, buf.at[slot_t], ypad_ref,
                          sem.at[slot_t], to_vmem=True)

        def zero_body(j, carry):
            buf[slot_t, pl.ds(pl.multiple_of(j * RUN_ALIGN, RUN_ALIGN), RUN_ALIGN), :] = jnp.zeros(
                (RUN_ALIGN, D_MODEL), F32)
            return carry
        lax.fori_loop(cnts_ref[2 * nt + t] // RUN_ALIGN, NCH, zero_body, 0)

    @pl.when(i == 0)
    def _():
        fetch(0, 0)

    @pl.when(i < nt - 1)
    def _():
        fetch(i + 1, 1 - slot_i)

    rt_cols = jnp.concatenate([rt_ref[...], jnp.zeros((LANES - SUBLANES, TS), F32)], axis=0).T
    p0c = rt_cols[:, 0:1]
    p1c = rt_cols[:, 1:2]
    w0c = rt_cols[:, 2:3]
    w1c = rt_cols[:, 3:4]
    lane_slot = lax.broadcasted_iota(jnp.int32, (TS, RL), 1).astype(F32)
    unsort = jnp.where(lane_slot == p0c, w0c, jnp.where(lane_slot == p1c, w1c, 0.0)).astype(BF16)

    _wait_run_copies(buf.at[slot_i], ypad_ref, sem.at[slot_i], cnts_ref[2 * nt + i], to_vmem=True)

    x2 = x1_ref[...] + jnp.dot(unsort, buf[slot_i].astype(BF16), preferred_element_type=F32)
    out_ref[...] = x2 * _rsqrt_mean_sq(x2) * gfin_ref[...]


def _combine(tabs, rt, x1, gfin, ypad):
    nt = rt.shape[0]
    d = x1.shape[-1]
    grid_spec = pltpu.PrefetchScalarGridSpec(
        num_scalar_prefetch=3,
        grid=(nt,),
        in_specs=[
            pl.BlockSpec((None, SUBLANES, TS), lambda i, *_: (i, 0, 0)),
            pl.BlockSpec((TS, d), lambda i, *_: (i, 0)),
            pl.BlockSpec((1, d), lambda i, *_: (0, 0)),
            pl.BlockSpec(memory_space=pl.ANY),
        ],
        out_specs=pl.BlockSpec((TS, d), lambda i, *_: (i, 0)),
        scratch_shapes=[pltpu.VMEM((2, RL, d), F32), pltpu.SemaphoreType.DMA((2,))],
    )
    return pl.pallas_call(
        _combine_kernel,
        grid_spec=grid_spec,
        out_shape=jax.ShapeDtypeStruct((nt * TS, d), F32),
        compiler_params=pltpu.CompilerParams(
            dimension_semantics=("arbitrary",), vmem_limit_bytes=VMEM_LIMIT),
        name="combine",
    )(tabs["big"], tabs["small"], tabs["cnts"], rt, x1, gfin, ypad)


def _compact(valid, values, width):
    v = valid.reshape(valid.shape[0], -1)
    _, out = lax.sort((~v, values.reshape(v.shape)), dimension=1, is_stable=True, num_keys=1)
    return out[:, :width].astype(jnp.int32).reshape(-1), jnp.sum(v, axis=1).astype(jnp.int32)


def _run_tables(cnt, cap):
    pc = cnt[:, :, 0].astype(jnp.int32)
    tot = jnp.sum(pc, axis=0)
    reg = ((tot + BR - 1) // BR) * BR
    reg_end = jnp.cumsum(reg)
    reg_start = reg_end - reg
    goff = reg_start[None, :] + jnp.cumsum(pc, axis=0) - pc
    lend = jnp.cumsum(pc, axis=1)
    lstart = lend - pc
    nbig_run = (pc // BIG)[:, :, None]
    jb = jnp.arange(TS // BIG, dtype=jnp.int32)[None, None, :]
    js = jnp.arange(BIG // RUN_ALIGN - 1, dtype=jnp.int32)[None, None, :]
    entry = (goff << LOCAL_BITS | lstart)[:, :, None]
    step = (1 << LOCAL_BITS) + 1
    big, nbig = _compact(jb < nbig_run, entry + jb * (BIG * step), NBIG)
    small, nsmall = _compact(js < ((pc % BIG) // RUN_ALIGN)[:, :, None],
                             entry + (nbig_run * BIG + js * RUN_ALIGN) * step, NSMALL)
    cnts = jnp.concatenate([nbig, nsmall, lend[:, -1]]).astype(jnp.int32)
    zs = reg_start + tot
    zhead = jnp.minimum(((zs + ZBIG - 1) // ZBIG) * ZBIG, reg_end)
    zrow_s = zs[:, None] + jnp.arange(ZBIG // RUN_ALIGN - 1, dtype=jnp.int32)[None, :] * RUN_ALIGN
    zrow_b = zhead[:, None] + jnp.arange(BR // ZBIG, dtype=jnp.int32)[None, :] * ZBIG
    zsmall, nzs = _compact((zrow_s < zhead[:, None])[None], zrow_s[None], zrow_s.size)
    zbig, nzb = _compact((zrow_b < reg_end[:, None])[None], zrow_b[None], zrow_b.size)
    nb = cap // BR
    blk = jnp.arange(nb, dtype=jnp.int32)
    block_e = jnp.minimum(jnp.sum(reg_end[None, :] <= (blk * BR)[:, None], axis=-1), N_EXPERTS - 1).astype(jnp.int32)
    nvalid = (reg_end[-1] // BR).astype(jnp.int32).reshape(1)
    first = jnp.concatenate([jnp.ones((1,), jnp.int32), (block_e[1:] != block_e[:-1]).astype(jnp.int32)])
    wslot = ((jnp.cumsum(first) - 1) % 2).astype(jnp.int32)
    eid = jnp.arange(N_EXPERTS, dtype=jnp.int32)
    later_used = (eid[None, :] > eid[:, None]) & (reg[None, :] > 0)
    nxt = jnp.min(jnp.where(later_used, eid[None, :], N_EXPERTS), axis=1)
    nxt = jnp.where(nxt < N_EXPERTS, nxt, -1)
    next_e = jnp.sum(jnp.where(block_e[:, None] == eid[None, :], nxt[None, :], 0), axis=1).astype(jnp.int32)
    return dict(big=big, small=small, cnts=cnts, zbig=zbig, zsmall=zsmall,
                zc=jnp.concatenate([nzs, nzb, nvalid]).astype(jnp.int32),
                block_e=block_e, first=first, wslot=wslot, next_e=next_e, nvalid=nvalid)


def kernel(x, norm_mix_g, w_in, b_in, conv_w, conv_b, conv_ln_g, conv_ln_b, w_conv_out, b_conv_out, sg_ln_g, sg_ln_b, w_sg, b_sg, w_sg_out, b_sg_out, w_out, b_out, norm_ffn_g, w_router_group, b_router_group, w_router_expert, b_router_expert, w_expert_gate, w_expert_up, w_expert_down, norm_final_g):
    bsz, seq, d = x.shape
    assert d == D_MODEL and seq % TS == 0 and w_in.shape[0] == 1
    nt = bsz * seq // TS
    n_tok = bsz * seq
    cap = -(-(2 * n_tok + nt * N_EXPERTS * (RUN_ALIGN - 1) + N_EXPERTS * (BR - RUN_ALIGN)) // BR) * BR

    row = lambda v: v.reshape(1, -1).astype(F32)
    wr = jnp.zeros((ROUTER_ROWS, d), F32)
    wr = wr.at[0:N_EXPERTS].set(w_router_expert[0].T).at[N_EXPERTS:N_EXPERTS + N_GROUPS].set(w_router_group[0].T)
    br = jnp.zeros((ROUTER_ROWS, 1), F32)
    br = br.at[0:N_EXPERTS, 0].set(b_router_expert[0]).at[N_EXPERTS:N_EXPERTS + N_GROUPS, 0].set(b_router_group[0])
    consts = [
        row(norm_mix_g[0]), w_in[0].astype(BF16), row(b_in[0]),
        conv_w[0], row(conv_b[0]), row(conv_ln_g[0]), row(conv_ln_b[0]),
        w_conv_out[0].astype(BF16), row(b_conv_out[0]),
        row(sg_ln_g[0]), row(sg_ln_b[0]), w_sg[0], b_sg[0].T,
        w_sg_out[0].astype(BF16), row(b_sg_out[0]),
        w_out[0].astype(BF16), row(b_out[0]), row(norm_ffn_g[0]),
        wr.astype(BF16), br,
        (jnp.arange(TS)[:, None] < jnp.arange(TS)[None, :]).astype(BF16),
    ]
    x1, h2, rt, cnt = _mixer(x, consts)
    tabs = _run_tables(cnt, cap)
    xpad = _dispatch(tabs, rt, h2.reshape(n_tok, d), cap)
    ypad = _experts(tabs, xpad, w_expert_gate[0], w_expert_up[0], w_expert_down[0])
    out = _combine(tabs, rt, x1.reshape(n_tok, d), row(norm_final_g), ypad)
    return out.reshape(bsz, seq, d)
```

```python
import jax
import jax.numpy as jnp
from jax import lax
from jax.experimental import pallas as pl
from jax.experimental.pallas import tpu as pltpu

F32 = jnp.float32
BF16 = jnp.bfloat16

D_MODEL = 1024
CONV_WIDTH = 31
SG_HEADS = 8
SG_BLOCK = 128
SG_CHUNK = 64
N_GROUPS = 4
EXPERTS_PER_GROUP = 8
N_EXPERTS = N_GROUPS * EXPERTS_PER_GROUP
D_EXPERT = D_MODEL // 2
EPS = 1e-6

SUBLANES = 8
LANES = 128
TS = 512
HALO = 32
ROWS = 16
STAT_ROWS = 256
CONV_ROWS = 64
NCOL = 512
RUN_ALIGN = SUBLANES
RL = -(-(2 * TS + N_EXPERTS * (RUN_ALIGN - 1)) // LANES) * LANES
NCH = RL // RUN_ALIGN
BIG = 32
NBIG = RL // BIG
NSMALL = N_EXPERTS * (BIG // RUN_ALIGN - 1)
LOCAL_BITS = 11
TPS = 2
BR = 1024
ZBIG = 64
ROUTER_ROWS = 64
NEG = -1e30
assert RL <= 1 << LOCAL_BITS
VMEM_LIMIT = 60 * 1024 * 1024


def _rsqrt_mean_sq(xf):
    return lax.rsqrt(jnp.mean(xf * xf, axis=-1, keepdims=True) + EPS)


def _layer_norm(xf, g, b):
    mu = jnp.mean(xf, axis=-1, keepdims=True)
    xc = xf - mu
    var = jnp.mean(xc * xc, axis=-1, keepdims=True)
    return xc * lax.rsqrt(var + EPS) * g + b


def _sigmoid(z):
    return 0.5 * jnp.tanh(0.5 * z) + 0.5


def _gelu(z):
    return 0.5 * z * (1.0 + lax.erf(z * (0.5 ** 0.5)))


def _rows(i, n):
    return pl.ds(pl.multiple_of(i * n, n), n)


def _matmul_cols(a_ref, w_ref, c0, n, out_ref):
    for j in range(0, n, NCOL):
        out_ref[:, j:j + NCOL] = jnp.dot(a_ref[...], w_ref[:, c0 + j:c0 + j + NCOL],
                                         preferred_element_type=F32)


def _mixer_kernel(x_ref, gmix_ref, win_ref, bin_ref, cw_ref, cb_ref, clg_ref, clb_ref, wco_ref, bco_ref,
                  slg_ref, slb_ref, wsg_ref, bsgt_ref, wso_ref, bso_ref, wout_ref, bout_ref, gffn_ref,
                  wr_ref, br_ref, before_ref,
                  x1_ref, h2_ref, rt_ref, cnt_ref,
                  h_s, p_s, a_s, c_s, ya_s, yb_s, u_s, v_s):
    d = D_MODEL
    s_idx = pl.program_id(1)

    def norm_body(i, carry):
        r = _rows(i, STAT_ROWS)
        xf = x_ref[r, :]
        h_s[r, :] = (xf * _rsqrt_mean_sq(xf) * gmix_ref[...]).astype(BF16)
        return carry
    lax.fori_loop(0, TS // STAT_ROWS, norm_body, 0)

    nlt = d // LANES

    @pl.when(s_idx == 0)
    def _():
        a_s[:, 0:HALO, :] = jnp.zeros((nlt, HALO, LANES), F32)

    @pl.when(s_idx > 0)
    def _():
        a_s[:, 0:HALO, :] = a_s[:, TS:TS + HALO, :]

    _matmul_cols(h_s, win_ref, 0, 2 * d, p_s)

    def glu_body(i, carry):
        r = _rows(i, ROWS)
        a1 = p_s[r, 0:d] + bin_ref[:, 0:d]
        a2 = p_s[r, d:2 * d] + bin_ref[:, d:2 * d]
        a = a1 * _sigmoid(a2)
        ra = pl.ds(pl.multiple_of(i * ROWS + HALO, ROWS), ROWS)
        for c in range(nlt):
            a_s[c, ra, :] = a[:, c * LANES:(c + 1) * LANES]
        return carry
    lax.fori_loop(0, TS // ROWS, glu_body, 0)

    base_off = HALO - (CONV_WIDTH - 1)

    def conv_body(i, carry):
        r0 = pl.multiple_of(i * CONV_ROWS, CONV_ROWS)
        for c in range(nlt):
            cs = slice(c * LANES, (c + 1) * LANES)
            acc = jnp.broadcast_to(cb_ref[:, cs], (CONV_ROWS, LANES))
            for k in range(CONV_WIDTH):
                acc = acc + cw_ref[k:k + 1, cs] * a_s[c, pl.ds(r0 + (base_off + k), CONV_ROWS), :]
            p_s[pl.ds(r0, CONV_ROWS), cs] = acc
        return carry
    lax.fori_loop(0, TS // CONV_ROWS, conv_body, 0)

    def conv_ln_body(i, carry):
        r = _rows(i, STAT_ROWS)
        y = _layer_norm(p_s[r, 0:d], clg_ref[...], clb_ref[...])
        c_s[r, :] = (y * _sigmoid(y)).astype(BF16)
        return carry
    lax.fori_loop(0, TS // STAT_ROWS, conv_ln_body, 0)

    _matmul_cols(c_s, wco_ref, 0, d, ya_s)

    _matmul_cols(h_s, win_ref, 2 * d, 2 * d, p_s)

    def gelu_body(i, carry):
        r = _rows(i, STAT_ROWS)
        z1 = p_s[r, 0:d] + bin_ref[:, 2 * d:3 * d]
        z2 = p_s[r, d:2 * d] + bin_ref[:, 3 * d:4 * d]
        u_s[r, :] = _gelu(z1)
        v_s[r, :] = _layer_norm(_gelu(z2), slg_ref[...], slb_ref[...]).astype(BF16)
        return carry
    lax.fori_loop(0, TS // STAT_ROWS, gelu_body, 0)

    qi = lax.broadcasted_iota(jnp.int32, (SG_BLOCK, SG_BLOCK), 0) // SG_CHUNK
    ki = lax.broadcasted_iota(jnp.int32, (SG_BLOCK, SG_BLOCK), 1) // SG_CHUNK
    causal = qi >= ki
    for hh in range(SG_HEADS):
        wm = jnp.where(causal, wsg_ref[hh], 0.0).astype(BF16)
        cs = slice(hh * SG_BLOCK, (hh + 1) * SG_BLOCK)
        for blk in range(TS // SG_BLOCK):
            rs = slice(blk * SG_BLOCK, (blk + 1) * SG_BLOCK)
            sv = jnp.dot(wm, v_s[rs, cs], preferred_element_type=F32) + bsgt_ref[:, hh:hh + 1]
            c_s[rs, cs] = (u_s[rs, cs] * sv).astype(BF16)

    _matmul_cols(c_s, wso_ref, 0, d, yb_s)

    _matmul_cols(h_s, win_ref, 4 * d, 2 * d, p_s)

    def gate_body(i, carry):
        r = _rows(i, ROWS)
        ga = _sigmoid(p_s[r, 0:d] + bin_ref[:, 4 * d:5 * d])
        gb = _sigmoid(p_s[r, d:2 * d] + bin_ref[:, 5 * d:6 * d])
        m = ga * (ya_s[r, :] + bco_ref[...]) + gb * (yb_s[r, :] + bso_ref[...])
        c_s[r, :] = m.astype(BF16)
        return carry
    lax.fori_loop(0, TS // ROWS, gate_body, 0)

    _matmul_cols(c_s, wout_ref, 0, d, ya_s)

    def resid_body(i, carry):
        r = _rows(i, STAT_ROWS)
        x1 = x_ref[r, :] + ya_s[r, :] + bout_ref[...]
        x1_ref[r, :] = x1
        h2_ref[r, :] = (x1 * _rsqrt_mean_sq(x1) * gffn_ref[...]).astype(BF16)
        return carry
    lax.fori_loop(0, TS // STAT_ROWS, resid_body, 0)

    lt = lax.dot_general(wr_ref[...], h2_ref[...], (((1,), (1,)), ((), ())),
                         preferred_element_type=F32) + br_ref[...]
    row8 = lax.broadcasted_iota(jnp.int32, (EXPERTS_PER_GROUP, TS), 0)
    gl = jnp.where(row8 < N_GROUPS, lt[N_EXPERTS:N_EXPERTS + 8], NEG)
    gmax = jnp.max(gl, axis=0, keepdims=True)
    gsel = jnp.min(jnp.where(gl == gmax, row8, 8), axis=0, keepdims=True)
    gw = 1.0 / jnp.sum(jnp.exp(gl - gmax), axis=0, keepdims=True)
    el = jnp.zeros((EXPERTS_PER_GROUP, TS), F32)
    for g in range(N_GROUPS):
        el = jnp.where(gsel == g, lt[g * EXPERTS_PER_GROUP:(g + 1) * EXPERTS_PER_GROUP], el)
    m1 = jnp.max(el, axis=0, keepdims=True)
    i1 = jnp.min(jnp.where(el == m1, row8, 8), axis=0, keepdims=True)
    el2 = jnp.where(row8 == i1, NEG, el)
    m2 = jnp.max(el2, axis=0, keepdims=True)
    i2 = jnp.min(jnp.where(el2 == m2, row8, 8), axis=0, keepdims=True)
    t = jnp.exp(m2 - m1)
    w0 = gw / (1.0 + t)
    w1 = gw * t / (1.0 + t)
    e0 = gsel * EXPERTS_PER_GROUP + i1
    e1 = gsel * EXPERTS_PER_GROUP + i2

    row32 = lax.broadcasted_iota(jnp.int32, (N_EXPERTS, TS), 0)
    oh0 = (row32 == e0).astype(F32)
    oh1 = (row32 == e1).astype(F32)
    c0 = jnp.sum(oh0, axis=1, keepdims=True)
    c1 = jnp.sum(oh1, axis=1, keepdims=True)
    padded = jnp.floor((c0 + c1 + (RUN_ALIGN - 1)) * (1.0 / RUN_ALIGN)) * RUN_ALIGN
    tri = (lax.broadcasted_iota(jnp.int32, (N_EXPERTS, N_EXPERTS), 0)
           > lax.broadcasted_iota(jnp.int32, (N_EXPERTS, N_EXPERTS), 1)).astype(BF16)
    units = jnp.broadcast_to(padded * (1.0 / RUN_ALIGN), (N_EXPERTS, LANES)).astype(BF16)
    lstart = RUN_ALIGN * jnp.dot(tri, units, preferred_element_type=F32)[:, 0:1]
    rank = jnp.dot(jnp.concatenate([oh0, oh1], axis=0).astype(BF16), before_ref[...],
                   preferred_element_type=F32)
    pos0 = jnp.sum(oh0 * (lstart + rank[0:N_EXPERTS]), axis=0, keepdims=True)
    pos1 = jnp.sum(oh1 * (lstart + c0 + rank[N_EXPERTS:]), axis=0, keepdims=True)
    rt_ref[...] = jnp.concatenate(
        [pos0, pos1, w0, w1, e0.astype(F32), e1.astype(F32), jnp.zeros((2, TS), F32)], axis=0)
    cnt_ref[...] = jnp.broadcast_to(padded, (N_EXPERTS, LANES))


def _const_spec(shape):
    nd = len(shape)
    return pl.BlockSpec(shape, lambda b, s: (0,) * nd, pipeline_mode=pl.Buffered(1))


def _mixer(x, consts):
    bsz, seq, d = x.shape
    ns = seq // TS
    nt = bsz * ns
    in_specs = [pl.BlockSpec((None, TS, d), lambda b, s: (b, s, 0))]
    in_specs += [_const_spec(c.shape) for c in consts]
    out_shape = (
        jax.ShapeDtypeStruct((bsz, seq, d), F32),
        jax.ShapeDtypeStruct((bsz, seq, d), BF16),
        jax.ShapeDtypeStruct((nt, SUBLANES, TS), F32),
        jax.ShapeDtypeStruct((nt, N_EXPERTS, LANES), F32),
    )
    out_specs = (
        pl.BlockSpec((None, TS, d), lambda b, s: (b, s, 0)),
        pl.BlockSpec((None, TS, d), lambda b, s: (b, s, 0)),
        pl.BlockSpec((None, SUBLANES, TS), lambda b, s: (b * ns + s, 0, 0)),
        pl.BlockSpec((None, N_EXPERTS, LANES), lambda b, s: (b * ns + s, 0, 0)),
    )
    scratch = [
        pltpu.VMEM((TS, d), BF16),
        pltpu.VMEM((TS, 2 * d), F32),
        pltpu.VMEM((d // LANES, TS + HALO, LANES), F32),
        pltpu.VMEM((TS, d), BF16),
        pltpu.VMEM((TS, d), F32),
        pltpu.VMEM((TS, d), F32),
        pltpu.VMEM((TS, d), F32),
        pltpu.VMEM((TS, d), BF16),
    ]
    return pl.pallas_call(
        _mixer_kernel,
        grid=(bsz, ns),
        in_specs=in_specs,
        out_specs=out_specs,
        out_shape=out_shape,
        scratch_shapes=scratch,
        compiler_params=pltpu.CompilerParams(
            dimension_semantics=("arbitrary", "arbitrary"), vmem_limit_bytes=VMEM_LIMIT),
        name="mixer",
    )(x, *consts)


def _run_copy(src, dst, sem, s0, d0, rows):
    return pltpu.make_async_copy(src.at[pl.ds(pl.multiple_of(s0, RUN_ALIGN), rows)],
                                 dst.at[pl.ds(pl.multiple_of(d0, RUN_ALIGN), rows)], sem)


def _start_run_copies(big_ref, small_ref, nbig, nsmall, t, vmem_buf, hbm_ref, sem, to_vmem):
    def start(entry, rows):
        local = entry & ((1 << LOCAL_BITS) - 1)
        glob = lax.shift_right_logical(entry, LOCAL_BITS)
        if to_vmem:
            _run_copy(hbm_ref, vmem_buf, sem, glob, local, rows).start()
        else:
            _run_copy(vmem_buf, hbm_ref, sem, local, glob, rows).start()

    def big_body(j, carry):
        start(big_ref[t * NBIG + j], BIG)
        return carry
    lax.fori_loop(0, nbig, big_body, 0)

    def small_body(j, carry):
        start(small_ref[t * NSMALL + j], RUN_ALIGN)
        return carry
    lax.fori_loop(0, nsmall, small_body, 0)


def _wait_run_copies(vmem_buf, hbm_ref, sem, rows, to_vmem):
    sl = pl.ds(0, pl.multiple_of(rows, RUN_ALIGN))
    if to_vmem:
        pltpu.make_async_copy(hbm_ref.at[sl], vmem_buf.at[sl], sem).wait()
    else:
        pltpu.make_async_copy(vmem_buf.at[sl], hbm_ref.at[sl], sem).wait()


def _dispatch_kernel(big_ref, small_ref, cnts_ref, zbig_ref, zsmall_ref, zc_ref, rt_ref, h2_ref, xpad_ref,
                     buf, zbuf, sem, zsem):
    i = pl.program_id(0)
    nsteps = pl.num_programs(0)
    nt = nsteps * TPS
    base = (i % 2) * TPS
    slot = lax.broadcasted_iota(jnp.int32, (RL, TS), 0).astype(F32)
    for k in range(TPS):
        t = i * TPS + k
        pos0 = rt_ref[k, 0:1, :]
        pos1 = rt_ref[k, 1:2, :]
        onehot = ((slot == pos0) | (slot == pos1)).astype(BF16)
        buf[base + k] = jnp.dot(onehot, h2_ref[k * TS:(k + 1) * TS, :], preferred_element_type=F32)
        _start_run_copies(big_ref, small_ref, cnts_ref[t], cnts_ref[nt + t], t, buf.at[base + k], xpad_ref,
                          sem.at[base + k], to_vmem=False)

    def wait_step(step, first_buf):
        for k in range(TPS):
            _wait_run_copies(buf.at[first_buf + k], xpad_ref, sem.at[first_buf + k],
                             cnts_ref[2 * nt + step * TPS + k], to_vmem=False)

    @pl.when(i > 0)
    def _():
        wait_step(i - 1, TPS - base)

    @pl.when(i == nsteps - 1)
    def _():
        zbuf[...] = jnp.zeros((BR, D_MODEL), F32)
        nb = xpad_ref.shape[0] // BR

        def zcopy(row, rows):
            return pltpu.make_async_copy(zbuf.at[pl.ds(0, rows)],
                                         xpad_ref.at[pl.ds(pl.multiple_of(row, RUN_ALIGN), rows)], zsem)

        def fill(tab_ref, count, rows):
            def issue(j, carry):
                zcopy(tab_ref[j], rows).start()
                return carry
            lax.fori_loop(0, count, issue, 0)

            def wait(j, carry):
                zcopy(0, rows).wait()
                return carry
            lax.fori_loop(0, count, wait, 0)

        fill(zsmall_ref, zc_ref[0], RUN_ALIGN)
        fill(zbig_ref, zc_ref[1], ZBIG)

        def bissue(b, carry):
            zcopy(b * BR, BR).start()
            return carry
        lax.fori_loop(zc_ref[2], nb, bissue, 0)

        def bwait(b, carry):
            zcopy(0, BR).wait()
            return carry
        lax.fori_loop(zc_ref[2], nb, bwait, 0)

        wait_step(i, base)


def _dispatch(tabs, rt, h2, cap):
    nt = rt.shape[0]
    d = h2.shape[-1]
    grid_spec = pltpu.PrefetchScalarGridSpec(
        num_scalar_prefetch=6,
        grid=(nt // TPS,),
        in_specs=[
            pl.BlockSpec((TPS, SUBLANES, TS), lambda i, *_: (i, 0, 0)),
            pl.BlockSpec((TPS * TS, d), lambda i, *_: (i, 0)),
        ],
        out_specs=pl.BlockSpec(memory_space=pl.ANY),
        scratch_shapes=[
            pltpu.VMEM((2 * TPS, RL, d), F32),
            pltpu.VMEM((BR, d), F32),
            pltpu.SemaphoreType.DMA((2 * TPS,)),
            pltpu.SemaphoreType.DMA,
        ],
    )
    return pl.pallas_call(
        _dispatch_kernel,
        grid_spec=grid_spec,
        out_shape=jax.ShapeDtypeStruct((cap, d), F32),
        compiler_params=pltpu.CompilerParams(
            dimension_semantics=("arbitrary",), vmem_limit_bytes=VMEM_LIMIT),
        name="dispatch",
    )(tabs["big"], tabs["small"], tabs["cnts"], tabs["zbig"], tabs["zsmall"], tabs["zc"], rt, h2)


def _expert_kernel(be_ref, first_ref, wslot_ref, nexte_ref, nv_ref, x_ref, wg_hbm, wu_hbm, wd_hbm, y_ref,
                   wg_f, wu_f, wd_f, wg_s, wu_s, wd_s, sem):
    b = pl.program_id(0)

    def weight_copies(e, ws):
        return (pltpu.make_async_copy(wg_hbm.at[e], wg_f.at[ws], sem.at[0, ws]),
                pltpu.make_async_copy(wu_hbm.at[e], wu_f.at[ws], sem.at[1, ws]),
                pltpu.make_async_copy(wd_hbm.at[e], wd_f.at[ws], sem.at[2, ws]))

    @pl.when((b < nv_ref[0]) & (first_ref[b] == 1))
    def _():
        e = be_ref[b]
        ws = wslot_ref[b]
        ne = nexte_ref[b]

        @pl.when(b == 0)
        def _():
            for c in weight_copies(e, ws):
                c.start()

        @pl.when(ne >= 0)
        def _():
            for c in weight_copies(ne, 1 - ws):
                c.start()

        for c in weight_copies(e, ws):
            c.wait()
        wg_s[...] = wg_f[ws].astype(BF16)
        wu_s[...] = wu_f[ws].astype(BF16)
        wd_s[...] = wd_f[ws].astype(BF16)

    @pl.when(b < nv_ref[0])
    def _():
        xb = x_ref[...].astype(BF16)
        g = jnp.dot(xb, wg_s[...], preferred_element_type=F32)
        u = jnp.dot(xb, wu_s[...], preferred_element_type=F32)
        hid = (g * _sigmoid(g) * u).astype(BF16)
        y_ref[...] = jnp.dot(hid, wd_s[...], preferred_element_type=F32)


def _experts(tabs, xpad, wg, wu, wd):
    cap, d = xpad.shape
    nb = cap // BR

    def row_map(b, be, fi, ws, ne, nv):
        return (jnp.minimum(b, nv[0] - 1), 0)

    grid_spec = pltpu.PrefetchScalarGridSpec(
        num_scalar_prefetch=5,
        grid=(nb,),
        in_specs=[
            pl.BlockSpec((BR, d), row_map),
            pl.BlockSpec(memory_space=pl.ANY),
            pl.BlockSpec(memory_space=pl.ANY),
            pl.BlockSpec(memory_space=pl.ANY),
        ],
        out_specs=pl.BlockSpec((BR, d), row_map),
        scratch_shapes=[
            pltpu.VMEM((2, d, D_EXPERT), F32),
            pltpu.VMEM((2, d, D_EXPERT), F32),
            pltpu.VMEM((2, D_EXPERT, d), F32),
            pltpu.VMEM((d, D_EXPERT), BF16),
            pltpu.VMEM((d, D_EXPERT), BF16),
            pltpu.VMEM((D_EXPERT, d), BF16),
            pltpu.SemaphoreType.DMA((3, 2)),
        ],
    )
    return pl.pallas_call(
        _expert_kernel,
        grid_spec=grid_spec,
        out_shape=jax.ShapeDtypeStruct((cap, d), F32),
        input_output_aliases={5: 0},
        compiler_params=pltpu.CompilerParams(
            dimension_semantics=("arbitrary",), vmem_limit_bytes=VMEM_LIMIT),
        name="experts",
    )(tabs["block_e"], tabs["first"], tabs["wslot"], tabs["next_e"], tabs["nvalid"], xpad, wg, wu, wd)


def _combine_kernel(big_ref, small_ref, cnts_ref, rt_ref, x1_ref, gfin_ref, ypad_ref, out_ref, buf, sem):
    i = pl.program_id(0)
    nsteps = pl.num_programs(0)
    nt = nsteps * TPS
    base = (i % 2) * TPS

    def fetch_step(step, first_buf):
        for k in range(TPS):
            t = step * TPS + k
            b = first_buf + k
            _start_run_copies(big_ref, small_ref, cnts_ref[t], cnts_ref[nt + t], t, buf.at[b], ypad_ref,
                              sem.at[b], to_vmem=True)

            def zero_body(j, carry, b=b):
                buf[b, pl.ds(pl.multiple_of(j * RUN_ALIGN, RUN_ALIGN), RUN_ALIGN), :] = jnp.zeros(
                    (RUN_ALIGN, D_MODEL), F32)
                return carry
            lax.fori_loop(cnts_ref[2 * nt + t] // RUN_ALIGN, NCH, zero_body, 0)

    @pl.when(i == 0)
    def _():
        fetch_step(0, 0)

    @pl.when(i < nsteps - 1)
    def _():
        fetch_step(i + 1, TPS - base)

    lane_slot = lax.broadcasted_iota(jnp.int32, (TS, RL), 1).astype(F32)
    for k in range(TPS):
        rt_cols = jnp.concatenate([rt_ref[k], jnp.zeros((LANES - SUBLANES, TS), F32)], axis=0).T
        p0c = rt_cols[:, 0:1]
        p1c = rt_cols[:, 1:2]
        w0c = rt_cols[:, 2:3]
        w1c = rt_cols[:, 3:4]
        unsort = jnp.where(lane_slot == p0c, w0c, jnp.where(lane_slot == p1c, w1c, 0.0)).astype(BF16)
        _wait_run_copies(buf.at[base + k], ypad_ref, sem.at[base + k], cnts_ref[2 * nt + i * TPS + k], to_vmem=True)
        rows = slice(k * TS, (k + 1) * TS)
        x2 = x1_ref[rows, :] + jnp.dot(unsort, buf[base + k].astype(BF16), preferred_element_type=F32)
        out_ref[rows, :] = x2 * _rsqrt_mean_sq(x2) * gfin_ref[...]


def _combine(tabs, rt, x1, gfin, ypad):
    nt = rt.shape[0]
    d = x1.shape[-1]
    grid_spec = pltpu.PrefetchScalarGridSpec(
        num_scalar_prefetch=3,
        grid=(nt // TPS,),
        in_specs=[
            pl.BlockSpec((TPS, SUBLANES, TS), lambda i, *_: (i, 0, 0)),
            pl.BlockSpec((TPS * TS, d), lambda i, *_: (i, 0)),
            pl.BlockSpec((1, d), lambda i, *_: (0, 0)),
            pl.BlockSpec(memory_space=pl.ANY),
        ],
        out_specs=pl.BlockSpec((TPS * TS, d), lambda i, *_: (i, 0)),
        scratch_shapes=[pltpu.VMEM((2 * TPS, RL, d), F32), pltpu.SemaphoreType.DMA((2 * TPS,))],
    )
    return pl.pallas_call(
        _combine_kernel,
        grid_spec=grid_spec,
        out_shape=jax.ShapeDtypeStruct((nt * TS, d), F32),
        compiler_params=pltpu.CompilerParams(
            dimension_semantics=("arbitrary",), vmem_limit_bytes=VMEM_LIMIT),
        name="combine",
    )(tabs["big"], tabs["small"], tabs["cnts"], rt, x1, gfin, ypad)


def _compact(valid, values, width):
    v = valid.reshape(valid.shape[0], -1)
    _, out = lax.sort((~v, values.reshape(v.shape)), dimension=1, is_stable=True, num_keys=1)
    return out[:, :width].astype(jnp.int32).reshape(-1), jnp.sum(v, axis=1).astype(jnp.int32)


def _run_tables(cnt, cap):
    pc = cnt[:, :, 0].astype(jnp.int32)
    tot = jnp.sum(pc, axis=0)
    reg = ((tot + BR - 1) // BR) * BR
    reg_end = jnp.cumsum(reg)
    reg_start = reg_end - reg
    goff = reg_start[None, :] + jnp.cumsum(pc, axis=0) - pc
    lend = jnp.cumsum(pc, axis=1)
    lstart = lend - pc
    nbig_run = (pc // BIG)[:, :, None]
    jb = jnp.arange(TS // BIG, dtype=jnp.int32)[None, None, :]
    js = jnp.arange(BIG // RUN_ALIGN - 1, dtype=jnp.int32)[None, None, :]
    entry = (goff << LOCAL_BITS | lstart)[:, :, None]
    step = (1 << LOCAL_BITS) + 1
    big, nbig = _compact(jb < nbig_run, entry + jb * (BIG * step), NBIG)
    small, nsmall = _compact(js < ((pc % BIG) // RUN_ALIGN)[:, :, None],
                             entry + (nbig_run * BIG + js * RUN_ALIGN) * step, NSMALL)
    cnts = jnp.concatenate([nbig, nsmall, lend[:, -1]]).astype(jnp.int32)
    zs = reg_start + tot
    zhead = jnp.minimum(((zs + ZBIG - 1) // ZBIG) * ZBIG, reg_end)
    zrow_s = zs[:, None] + jnp.arange(ZBIG // RUN_ALIGN - 1, dtype=jnp.int32)[None, :] * RUN_ALIGN
    zrow_b = zhead[:, None] + jnp.arange(BR // ZBIG, dtype=jnp.int32)[None, :] * ZBIG
    zsmall, nzs = _compact((zrow_s < zhead[:, None])[None], zrow_s[None], zrow_s.size)
    zbig, nzb = _compact((zrow_b < reg_end[:, None])[None], zrow_b[None], zrow_b.size)
    nb = cap // BR
    blk = jnp.arange(nb, dtype=jnp.int32)
    block_e = jnp.minimum(jnp.sum(reg_end[None, :] <= (blk * BR)[:, None], axis=-1), N_EXPERTS - 1).astype(jnp.int32)
    nvalid = (reg_end[-1] // BR).astype(jnp.int32).reshape(1)
    first = jnp.concatenate([jnp.ones((1,), jnp.int32), (block_e[1:] != block_e[:-1]).astype(jnp.int32)])
    wslot = ((jnp.cumsum(first) - 1) % 2).astype(jnp.int32)
    eid = jnp.arange(N_EXPERTS, dtype=jnp.int32)
    later_used = (eid[None, :] > eid[:, None]) & (reg[None, :] > 0)
    nxt = jnp.min(jnp.where(later_used, eid[None, :], N_EXPERTS), axis=1)
    nxt = jnp.where(nxt < N_EXPERTS, nxt, -1)
    next_e = jnp.sum(jnp.where(block_e[:, None] == eid[None, :], nxt[None, :], 0), axis=1).astype(jnp.int32)
    return dict(big=big, small=small, cnts=cnts, zbig=zbig, zsmall=zsmall,
                zc=jnp.concatenate([nzs, nzb, nvalid]).astype(jnp.int32),
                block_e=block_e, first=first, wslot=wslot, next_e=next_e, nvalid=nvalid)


def kernel(x, norm_mix_g, w_in, b_in, conv_w, conv_b, conv_ln_g, conv_ln_b, w_conv_out, b_conv_out, sg_ln_g, sg_ln_b, w_sg, b_sg, w_sg_out, b_sg_out, w_out, b_out, norm_ffn_g, w_router_group, b_router_group, w_router_expert, b_router_expert, w_expert_gate, w_expert_up, w_expert_down, norm_final_g):
    bsz, seq, d = x.shape
    assert d == D_MODEL and seq % TS == 0 and w_in.shape[0] == 1 and (bsz * seq // TS) % TPS == 0
    nt = bsz * seq // TS
    n_tok = bsz * seq
    cap = -(-(2 * n_tok + nt * N_EXPERTS * (RUN_ALIGN - 1) + N_EXPERTS * (BR - RUN_ALIGN)) // BR) * BR

    row = lambda v: v.reshape(1, -1).astype(F32)
    wr = jnp.zeros((ROUTER_ROWS, d), F32)
    wr = wr.at[0:N_EXPERTS].set(w_router_expert[0].T).at[N_EXPERTS:N_EXPERTS + N_GROUPS].set(w_router_group[0].T)
    br = jnp.zeros((ROUTER_ROWS, 1), F32)
    br = br.at[0:N_EXPERTS, 0].set(b_router_expert[0]).at[N_EXPERTS:N_EXPERTS + N_GROUPS, 0].set(b_router_group[0])
    consts = [
        row(norm_mix_g[0]), w_in[0].astype(BF16), row(b_in[0]),
        conv_w[0], row(conv_b[0]), row(conv_ln_g[0]), row(conv_ln_b[0]),
        w_conv_out[0].astype(BF16), row(b_conv_out[0]),
        row(sg_ln_g[0]), row(sg_ln_b[0]), w_sg[0], b_sg[0].T,
        w_sg_out[0].astype(BF16), row(b_sg_out[0]),
        w_out[0].astype(BF16), row(b_out[0]), row(norm_ffn_g[0]),
        wr.astype(BF16), br,
        (jnp.arange(TS)[:, None] < jnp.arange(TS)[None, :]).astype(BF16),
    ]
    x1, h2, rt, cnt = _mixer(x, consts)
    tabs = _run_tables(cnt, cap)
    xpad = _dispatch(tabs, rt, h2.reshape(n_tok, d), cap)
    ypad = _experts(tabs, xpad, w_expert_gate[0], w_expert_up[0], w_expert_down[0])
    out = _combine(tabs, rt, x1.reshape(n_tok, d), row(norm_final_g), ypad)
    return out.reshape(bsz, seq, d)
```

```python
import math

import jax
import jax.numpy as jnp
from jax import lax
from jax.experimental import pallas as pl
from jax.experimental.pallas import tpu as pltpu

F32 = jnp.float32
BF16 = jnp.bfloat16

D_MODEL = 1024
CONV_WIDTH = 31
SG_HEADS = 8
SG_BLOCK = 128
SG_CHUNK = 64
N_GROUPS = 4
EXPERTS_PER_GROUP = 8
N_EXPERTS = N_GROUPS * EXPERTS_PER_GROUP
D_EXPERT = D_MODEL // 2
EPS = 1e-6

SUBLANES = 8
LANES = 128
TS = 512
HALO = 32
ROWS = TS
STAT_ROWS = TS
CONV_ROWS = 128
NCOL = 512
RUN_ALIGN = SUBLANES
RL = -(-(2 * TS + N_EXPERTS * (RUN_ALIGN - 1)) // LANES) * LANES
NCH = RL // RUN_ALIGN
BIG = 32
NBIG = RL // BIG
NSMALL = N_EXPERTS * (BIG // RUN_ALIGN - 1)
LOCAL_BITS = 11
TPS = 2
BR = 1024
ZBIG = 64
ROUTER_ROWS = 64
NEG = -1e30
assert RL <= 1 << LOCAL_BITS
VMEM_LIMIT = 60 * 1024 * 1024


def _rsqrt_mean_sq(xf):
    return lax.rsqrt(jnp.mean(xf * xf, axis=-1, keepdims=True) + EPS)


def _layer_norm(xf, g, b):
    mu = jnp.mean(xf, axis=-1, keepdims=True)
    xc = xf - mu
    var = jnp.mean(xc * xc, axis=-1, keepdims=True)
    return xc * lax.rsqrt(var + EPS) * g + b


def _sigmoid(z):
    return 0.5 * jnp.tanh(0.5 * z) + 0.5


def _gelu(z):
    return 0.5 * z * (1.0 + lax.erf(z * (0.5 ** 0.5)))


def _rows(i, n):
    return pl.ds(pl.multiple_of(i * n, n), n)


def _matmul_cols(a_ref, w_ref, c0, n, out_ref):
    for j in range(0, n, NCOL):
        out_ref[:, j:j + NCOL] = jnp.dot(a_ref[...], w_ref[:, c0 + j:c0 + j + NCOL],
                                         preferred_element_type=F32)


def _mixer_kernel(x_ref, gmix_ref, win_ref, bin_ref, cw_ref, cb_ref, clg_ref, clb_ref, wco_ref, bco_ref,
                  slg_ref, slb_ref, wsg_ref, bsgt_ref, wso_ref, bso_ref, wout_ref, bout_ref, gffn_ref,
                  wr_ref, br_ref, before_ref,
                  x1_ref, h2_ref, rt_ref, cnt_ref,
                  h_s, p_s, pz_s, pg_s, a_s, c_s, ya_s, yb_s, u_s, v_s):
    d = D_MODEL
    s_idx = pl.program_id(1)

    nlt = d // LANES

    @pl.when(s_idx == 0)
    def _():
        a_s[:, 0:HALO, :] = jnp.zeros((nlt, HALO, LANES), F32)

    @pl.when(s_idx > 0)
    def _():
        a_s[:, 0:HALO, :] = a_s[:, TS:TS + HALO, :]

    def norm_body(i, carry):
        r = _rows(i, STAT_ROWS)
        xf = x_ref[r, :]
        h_s[r, :] = (xf * _rsqrt_mean_sq(xf) * gmix_ref[...]).astype(BF16)
        return carry
    lax.fori_loop(0, TS // STAT_ROWS, norm_body, 0)

    _matmul_cols(h_s, win_ref, 0, 2 * d, p_s)

    def glu_body(i, carry):
        r = _rows(i, ROWS)
        a1 = p_s[r, 0:d] + bin_ref[:, 0:d]
        a2 = p_s[r, d:2 * d] + bin_ref[:, d:2 * d]
        a = a1 * _sigmoid(a2)
        ra = pl.ds(pl.multiple_of(i * ROWS + HALO, math.gcd(ROWS, HALO)), ROWS)
        for c in range(nlt):
            a_s[c, ra, :] = a[:, c * LANES:(c + 1) * LANES]
        return carry
    lax.fori_loop(0, TS // ROWS, glu_body, 0)

    base_off = HALO - (CONV_WIDTH - 1)

    _matmul_cols(h_s, win_ref, 2 * d, 2 * d, pz_s)
    _matmul_cols(h_s, win_ref, 4 * d, 2 * d, pg_s)
    for r0 in range(0, TS, CONV_ROWS):
        for c in range(nlt):
            cs = slice(c * LANES, (c + 1) * LANES)
            acc = jnp.broadcast_to(cb_ref[:, cs], (CONV_ROWS, LANES))
            for k in range(CONV_WIDTH):
                acc = acc + cw_ref[k:k + 1, cs] * a_s[c, r0 + base_off + k:r0 + base_off + k + CONV_ROWS, :]
            p_s[r0:r0 + CONV_ROWS, cs] = acc

    def conv_ln_body(i, carry):
        r = _rows(i, STAT_ROWS)
        y = _layer_norm(p_s[r, 0:d], clg_ref[...], clb_ref[...])
        c_s[r, :] = (y * _sigmoid(y)).astype(BF16)
        return carry
    lax.fori_loop(0, TS // STAT_ROWS, conv_ln_body, 0)

    _matmul_cols(c_s, wco_ref, 0, d, ya_s)


    def gelu_u_body(i, carry):
        r = _rows(i, ROWS)
        u_s[r, :] = _gelu(pz_s[r, 0:d] + bin_ref[:, 2 * d:3 * d])
        return carry
    lax.fori_loop(0, TS // ROWS, gelu_u_body, 0)

    def gelu_v_body(i, carry):
        r = _rows(i, STAT_ROWS)
        z2 = pz_s[r, d:2 * d] + bin_ref[:, 3 * d:4 * d]
        v_s[r, :] = _layer_norm(_gelu(z2), slg_ref[...], slb_ref[...]).astype(BF16)
        return carry
    lax.fori_loop(0, TS // STAT_ROWS, gelu_v_body, 0)

    qi = lax.broadcasted_iota(jnp.int32, (SG_BLOCK, SG_BLOCK), 0) // SG_CHUNK
    ki = lax.broadcasted_iota(jnp.int32, (SG_BLOCK, SG_BLOCK), 1) // SG_CHUNK
    causal = qi >= ki
    for hh in range(SG_HEADS):
        wm = jnp.where(causal, wsg_ref[hh], 0.0).astype(BF16)
        cs = slice(hh * SG_BLOCK, (hh + 1) * SG_BLOCK)
        for blk in range(TS // SG_BLOCK):
            rs = slice(blk * SG_BLOCK, (blk + 1) * SG_BLOCK)
            sv = jnp.dot(wm, v_s[rs, cs], preferred_element_type=F32) + bsgt_ref[:, hh:hh + 1]
            c_s[rs, cs] = (u_s[rs, cs] * sv).astype(BF16)

    _matmul_cols(c_s, wso_ref, 0, d, yb_s)


    def gate_body(i, carry):
        r = _rows(i, ROWS)
        ga = _sigmoid(pg_s[r, 0:d] + bin_ref[:, 4 * d:5 * d])
        gb = _sigmoid(pg_s[r, d:2 * d] + bin_ref[:, 5 * d:6 * d])
        m = ga * (ya_s[r, :] + bco_ref[...]) + gb * (yb_s[r, :] + bso_ref[...])
        c_s[r, :] = m.astype(BF16)
        return carry
    lax.fori_loop(0, TS // ROWS, gate_body, 0)

    _matmul_cols(c_s, wout_ref, 0, d, ya_s)

    def resid_body(i, carry):
        r = _rows(i, STAT_ROWS)
        x1 = x_ref[r, :] + ya_s[r, :] + bout_ref[...]
        x1_ref[r, :] = x1
        h2_ref[r, :] = (x1 * _rsqrt_mean_sq(x1) * gffn_ref[...]).astype(BF16)
        return carry
    lax.fori_loop(0, TS // STAT_ROWS, resid_body, 0)

    lt = lax.dot_general(wr_ref[...], h2_ref[...], (((1,), (1,)), ((), ())),
                         preferred_element_type=F32) + br_ref[...]
    row8 = lax.broadcasted_iota(jnp.int32, (EXPERTS_PER_GROUP, TS), 0)
    gl = jnp.where(row8 < N_GROUPS, lt[N_EXPERTS:N_EXPERTS + 8], NEG)
    gmax = jnp.max(gl, axis=0, keepdims=True)
    gsel = jnp.min(jnp.where(gl == gmax, row8, 8), axis=0, keepdims=True)
    gw = 1.0 / jnp.sum(jnp.exp(gl - gmax), axis=0, keepdims=True)
    el = jnp.zeros((EXPERTS_PER_GROUP, TS), F32)
    for g in range(N_GROUPS):
        el = jnp.where(gsel == g, lt[g * EXPERTS_PER_GROUP:(g + 1) * EXPERTS_PER_GROUP], el)
    m1 = jnp.max(el, axis=0, keepdims=True)
    i1 = jnp.min(jnp.where(el == m1, row8, 8), axis=0, keepdims=True)
    el2 = jnp.where(row8 == i1, NEG, el)
    m2 = jnp.max(el2, axis=0, keepdims=True)
    i2 = jnp.min(jnp.where(el2 == m2, row8, 8), axis=0, keepdims=True)
    t = jnp.exp(m2 - m1)
    w0 = gw / (1.0 + t)
    w1 = gw * t / (1.0 + t)
    e0 = gsel * EXPERTS_PER_GROUP + i1
    e1 = gsel * EXPERTS_PER_GROUP + i2

    row32 = lax.broadcasted_iota(jnp.int32, (N_EXPERTS, TS), 0)
    oh0 = (row32 == e0).astype(F32)
    oh1 = (row32 == e1).astype(F32)
    c0 = jnp.sum(oh0, axis=1, keepdims=True)
    c1 = jnp.sum(oh1, axis=1, keepdims=True)
    padded = jnp.floor((c0 + c1 + (RUN_ALIGN - 1)) * (1.0 / RUN_ALIGN)) * RUN_ALIGN
    tri = (lax.broadcasted_iota(jnp.int32, (N_EXPERTS, N_EXPERTS), 0)
           > lax.broadcasted_iota(jnp.int32, (N_EXPERTS, N_EXPERTS), 1)).astype(BF16)
    units = jnp.broadcast_to(padded * (1.0 / RUN_ALIGN), (N_EXPERTS, LANES)).astype(BF16)
    lstart = RUN_ALIGN * jnp.dot(tri, units, preferred_element_type=F32)[:, 0:1]
    rank = jnp.dot(jnp.concatenate([oh0, oh1], axis=0).astype(BF16), before_ref[...],
                   preferred_element_type=F32)
    pos0 = jnp.sum(oh0 * (lstart + rank[0:N_EXPERTS]), axis=0, keepdims=True)
    pos1 = jnp.sum(oh1 * (lstart + c0 + rank[N_EXPERTS:]), axis=0, keepdims=True)
    rt_ref[...] = jnp.concatenate(
        [pos0, pos1, w0, w1, e0.astype(F32), e1.astype(F32), jnp.zeros((2, TS), F32)], axis=0)
    cnt_ref[...] = jnp.broadcast_to(padded, (N_EXPERTS, LANES))


def _const_spec(shape):
    nd = len(shape)
    return pl.BlockSpec(shape, lambda b, s: (0,) * nd, pipeline_mode=pl.Buffered(1))


def _mixer(x, consts):
    bsz, seq, d = x.shape
    ns = seq // TS
    nt = bsz * ns
    in_specs = [pl.BlockSpec((None, TS, d), lambda b, s: (b, s, 0))]
    in_specs += [_const_spec(c.shape) for c in consts]
    out_shape = (
        jax.ShapeDtypeStruct((bsz, seq, d), F32),
        jax.ShapeDtypeStruct((bsz, seq, d), BF16),
        jax.ShapeDtypeStruct((nt, SUBLANES, TS), F32),
        jax.ShapeDtypeStruct((nt, N_EXPERTS, LANES), F32),
    )
    out_specs = (
        pl.BlockSpec((None, TS, d), lambda b, s: (b, s, 0)),
        pl.BlockSpec((None, TS, d), lambda b, s: (b, s, 0)),
        pl.BlockSpec((None, SUBLANES, TS), lambda b, s: (b * ns + s, 0, 0)),
        pl.BlockSpec((None, N_EXPERTS, LANES), lambda b, s: (b * ns + s, 0, 0)),
    )
    scratch = [
        pltpu.VMEM((TS, d), BF16),
        pltpu.VMEM((TS, 2 * d), F32),
        pltpu.VMEM((TS, 2 * d), F32),
        pltpu.VMEM((TS, 2 * d), F32),
        pltpu.VMEM((d // LANES, TS + HALO, LANES), F32),
        pltpu.VMEM((TS, d), BF16),
        pltpu.VMEM((TS, d), F32),
        pltpu.VMEM((TS, d), F32),
        pltpu.VMEM((TS, d), F32),
        pltpu.VMEM((TS, d), BF16),
    ]
    return pl.pallas_call(
        _mixer_kernel,
        grid=(bsz, ns),
        in_specs=in_specs,
        out_specs=out_specs,
        out_shape=out_shape,
        scratch_shapes=scratch,
        compiler_params=pltpu.CompilerParams(
            dimension_semantics=("arbitrary", "arbitrary"), vmem_limit_bytes=VMEM_LIMIT),
        name="mixer",
    )(x, *consts)


def _run_copy(src, dst, sem, s0, d0, rows):
    return pltpu.make_async_copy(src.at[pl.ds(pl.multiple_of(s0, RUN_ALIGN), rows)],
                                 dst.at[pl.ds(pl.multiple_of(d0, RUN_ALIGN), rows)], sem)


def _start_run_copies(big_ref, small_ref, nbig, nsmall, t, vmem_buf, hbm_ref, sem, to_vmem):
    def start(entry, rows):
        local = entry & ((1 << LOCAL_BITS) - 1)
        glob = lax.shift_right_logical(entry, LOCAL_BITS)
        if to_vmem:
            _run_copy(hbm_ref, vmem_buf, sem, glob, local, rows).start()
        else:
            _run_copy(vmem_buf, hbm_ref, sem, local, glob, rows).start()

    def big_body(j, carry):
        start(big_ref[t * NBIG + j], BIG)
        return carry
    lax.fori_loop(0, nbig, big_body, 0)

    def small_body(j, carry):
        start(small_ref[t * NSMALL + j], RUN_ALIGN)
        return carry
    lax.fori_loop(0, nsmall, small_body, 0)


def _wait_run_copies(vmem_buf, hbm_ref, sem, rows, to_vmem):
    sl = pl.ds(0, pl.multiple_of(rows, RUN_ALIGN))
    if to_vmem:
        pltpu.make_async_copy(hbm_ref.at[sl], vmem_buf.at[sl], sem).wait()
    else:
        pltpu.make_async_copy(vmem_buf.at[sl], hbm_ref.at[sl], sem).wait()


def _dispatch_kernel(big_ref, small_ref, cnts_ref, zbig_ref, zsmall_ref, zc_ref, rt_ref, h2_ref, xpad_ref,
                     buf, zbuf, sem, zsem):
    i = pl.program_id(0)
    nsteps = pl.num_programs(0)
    nt = nsteps * TPS
    base = (i % 2) * TPS
    slot = lax.broadcasted_iota(jnp.int32, (RL, TS), 0).astype(F32)
    for k in range(TPS):
        t = i * TPS + k
        pos0 = rt_ref[k, 0:1, :]
        pos1 = rt_ref[k, 1:2, :]
        onehot = ((slot == pos0) | (slot == pos1)).astype(BF16)
        buf[base + k] = jnp.dot(onehot, h2_ref[k * TS:(k + 1) * TS, :], preferred_element_type=F32)
        _start_run_copies(big_ref, small_ref, cnts_ref[t], cnts_ref[nt + t], t, buf.at[base + k], xpad_ref,
                          sem.at[base + k], to_vmem=False)

    def wait_step(step, first_buf):
        for k in range(TPS):
            _wait_run_copies(buf.at[first_buf + k], xpad_ref, sem.at[first_buf + k],
                             cnts_ref[2 * nt + step * TPS + k], to_vmem=False)

    @pl.when(i > 0)
    def _():
        wait_step(i - 1, TPS - base)

    @pl.when(i == nsteps - 1)
    def _():
        zbuf[...] = jnp.zeros((BR, D_MODEL), F32)
        nb = xpad_ref.shape[0] // BR

        def zcopy(row, rows):
            return pltpu.make_async_copy(zbuf.at[pl.ds(0, rows)],
                                         xpad_ref.at[pl.ds(pl.multiple_of(row, RUN_ALIGN), rows)], zsem)

        def fill(tab_ref, count, rows):
            def issue(j, carry):
                zcopy(tab_ref[j], rows).start()
                return carry
            lax.fori_loop(0, count, issue, 0)

            def wait(j, carry):
                zcopy(0, rows).wait()
                return carry
            lax.fori_loop(0, count, wait, 0)

        fill(zsmall_ref, zc_ref[0], RUN_ALIGN)
        fill(zbig_ref, zc_ref[1], ZBIG)

        def bissue(b, carry):
            zcopy(b * BR, BR).start()
            return carry
        lax.fori_loop(zc_ref[2], nb, bissue, 0)

        def bwait(b, carry):
            zcopy(0, BR).wait()
            return carry
        lax.fori_loop(zc_ref[2], nb, bwait, 0)

        wait_step(i, base)


def _dispatch(tabs, rt, h2, cap):
    nt = rt.shape[0]
    d = h2.shape[-1]
    grid_spec = pltpu.PrefetchScalarGridSpec(
        num_scalar_prefetch=6,
        grid=(nt // TPS,),
        in_specs=[
            pl.BlockSpec((TPS, SUBLANES, TS), lambda i, *_: (i, 0, 0)),
            pl.BlockSpec((TPS * TS, d), lambda i, *_: (i, 0)),
        ],
        out_specs=pl.BlockSpec(memory_space=pl.ANY),
        scratch_shapes=[
            pltpu.VMEM((2 * TPS, RL, d), F32),
            pltpu.VMEM((BR, d), F32),
            pltpu.SemaphoreType.DMA((2 * TPS,)),
            pltpu.SemaphoreType.DMA,
        ],
    )
    return pl.pallas_call(
        _dispatch_kernel,
        grid_spec=grid_spec,
        out_shape=jax.ShapeDtypeStruct((cap, d), F32),
        compiler_params=pltpu.CompilerParams(
            dimension_semantics=("arbitrary",), vmem_limit_bytes=VMEM_LIMIT),
        name="dispatch",
    )(tabs["big"], tabs["small"], tabs["cnts"], tabs["zbig"], tabs["zsmall"], tabs["zc"], rt, h2)


def _expert_kernel(be_ref, first_ref, wslot_ref, nexte_ref, nv_ref, x_ref, wg_hbm, wu_hbm, wd_hbm, y_ref,
                   wg_f, wu_f, wd_f, wg_s, wu_s, wd_s, sem):
    b = pl.program_id(0)

    def weight_copies(e, ws):
        return (pltpu.make_async_copy(wg_hbm.at[e], wg_f.at[ws], sem.at[0, ws]),
                pltpu.make_async_copy(wu_hbm.at[e], wu_f.at[ws], sem.at[1, ws]),
                pltpu.make_async_copy(wd_hbm.at[e], wd_f.at[ws], sem.at[2, ws]))

    @pl.when((b < nv_ref[0]) & (first_ref[b] == 1))
    def _():
        e = be_ref[b]
        ws = wslot_ref[b]
        ne = nexte_ref[b]

        @pl.when(b == 0)
        def _():
            for c in weight_copies(e, ws):
                c.start()

        @pl.when(ne >= 0)
        def _():
            for c in weight_copies(ne, 1 - ws):
                c.start()

        for c in weight_copies(e, ws):
            c.wait()
        wg_s[...] = wg_f[ws].astype(BF16)
        wu_s[...] = wu_f[ws].astype(BF16)
        wd_s[...] = wd_f[ws].astype(BF16)

    @pl.when(b < nv_ref[0])
    def _():
        xb = x_ref[...].astype(BF16)
        g = jnp.dot(xb, wg_s[...], preferred_element_type=F32)
        u = jnp.dot(xb, wu_s[...], preferred_element_type=F32)
        hid = (g * _sigmoid(g) * u).astype(BF16)
        y_ref[...] = jnp.dot(hid, wd_s[...], preferred_element_type=F32)


def _experts(tabs, xpad, wg, wu, wd):
    cap, d = xpad.shape
    nb = cap // BR

    def row_map(b, be, fi, ws, ne, nv):
        return (jnp.minimum(b, nv[0] - 1), 0)

    grid_spec = pltpu.PrefetchScalarGridSpec(
        num_scalar_prefetch=5,
        grid=(nb,),
        in_specs=[
            pl.BlockSpec((BR, d), row_map),
            pl.BlockSpec(memory_space=pl.ANY),
            pl.BlockSpec(memory_space=pl.ANY),
            pl.BlockSpec(memory_space=pl.ANY),
        ],
        out_specs=pl.BlockSpec((BR, d), row_map),
        scratch_shapes=[
            pltpu.VMEM((2, d, D_EXPERT), F32),
            pltpu.VMEM((2, d, D_EXPERT), F32),
            pltpu.VMEM((2, D_EXPERT, d), F32),
            pltpu.VMEM((d, D_EXPERT), BF16),
            pltpu.VMEM((d, D_EXPERT), BF16),
            pltpu.VMEM((D_EXPERT, d), BF16),
            pltpu.SemaphoreType.DMA((3, 2)),
        ],
    )
    return pl.pallas_call(
        _expert_kernel,
        grid_spec=grid_spec,
        out_shape=jax.ShapeDtypeStruct((cap, d), F32),
        input_output_aliases={5: 0},
        compiler_params=pltpu.CompilerParams(
            dimension_semantics=("arbitrary",), vmem_limit_bytes=VMEM_LIMIT),
        name="experts",
    )(tabs["block_e"], tabs["first"], tabs["wslot"], tabs["next_e"], tabs["nvalid"], xpad, wg, wu, wd)


def _combine_kernel(big_ref, small_ref, cnts_ref, rt_ref, x1_ref, gfin_ref, ypad_ref, out_ref, buf, sem):
    i = pl.program_id(0)
    nsteps = pl.num_programs(0)
    nt = nsteps * TPS
    base = (i % 2) * TPS

    def fetch_step(step, first_buf):
        for k in range(TPS):
            t = step * TPS + k
            b = first_buf + k
            _start_run_copies(big_ref, small_ref, cnts_ref[t], cnts_ref[nt + t], t, buf.at[b], ypad_ref,
                              sem.at[b], to_vmem=True)

            def zero_body(j, carry, b=b):
                buf[b, pl.ds(pl.multiple_of(j * RUN_ALIGN, RUN_ALIGN), RUN_ALIGN), :] = jnp.zeros(
                    (RUN_ALIGN, D_MODEL), F32)
                return carry
            lax.fori_loop(cnts_ref[2 * nt + t] // RUN_ALIGN, NCH, zero_body, 0)

    @pl.when(i == 0)
    def _():
        fetch_step(0, 0)

    @pl.when(i < nsteps - 1)
    def _():
        fetch_step(i + 1, TPS - base)

    lane_slot = lax.broadcasted_iota(jnp.int32, (TS, RL), 1).astype(F32)
    for k in range(TPS):
        rt_cols = jnp.concatenate([rt_ref[k], jnp.zeros((LANES - SUBLANES, TS), F32)], axis=0).T
        p0c = rt_cols[:, 0:1]
        p1c = rt_cols[:, 1:2]
        w0c = rt_cols[:, 2:3]
        w1c = rt_cols[:, 3:4]
        unsort = jnp.where(lane_slot == p0c, w0c, jnp.where(lane_slot == p1c, w1c, 0.0)).astype(BF16)
        _wait_run_copies(buf.at[base + k], ypad_ref, sem.at[base + k], cnts_ref[2 * nt + i * TPS + k], to_vmem=True)
        rows = slice(k * TS, (k + 1) * TS)
        x2 = x1_ref[rows, :] + jnp.dot(unsort, buf[base + k].astype(BF16), preferred_element_type=F32)
        out_ref[rows, :] = x2 * _rsqrt_mean_sq(x2) * gfin_ref[...]


def _combine(tabs, rt, x1, gfin, ypad):
    nt = rt.shape[0]
    d = x1.shape[-1]
    grid_spec = pltpu.PrefetchScalarGridSpec(
        num_scalar_prefetch=3,
        grid=(nt // TPS,),
        in_specs=[
            pl.BlockSpec((TPS, SUBLANES, TS), lambda i, *_: (i, 0, 0)),
            pl.BlockSpec((TPS * TS, d), lambda i, *_: (i, 0)),
            pl.BlockSpec((1, d), lambda i, *_: (0, 0)),
            pl.BlockSpec(memory_space=pl.ANY),
        ],
        out_specs=pl.BlockSpec((TPS * TS, d), lambda i, *_: (i, 0)),
        scratch_shapes=[pltpu.VMEM((2 * TPS, RL, d), F32), pltpu.SemaphoreType.DMA((2 * TPS,))],
    )
    return pl.pallas_call(
        _combine_kernel,
        grid_spec=grid_spec,
        out_shape=jax.ShapeDtypeStruct((nt * TS, d), F32),
        compiler_params=pltpu.CompilerParams(
            dimension_semantics=("arbitrary",), vmem_limit_bytes=VMEM_LIMIT),
        name="combine",
    )(tabs["big"], tabs["small"], tabs["cnts"], rt, x1, gfin, ypad)


def _compact(valid, values, width):
    v = valid.reshape(valid.shape[0], -1)
    _, out = lax.sort((~v, values.reshape(v.shape)), dimension=1, is_stable=True, num_keys=1)
    return out[:, :width].astype(jnp.int32).reshape(-1), jnp.sum(v, axis=1).astype(jnp.int32)


def _run_tables(cnt, cap):
    pc = cnt[:, :, 0].astype(jnp.int32)
    tot = jnp.sum(pc, axis=0)
    reg = ((tot + BR - 1) // BR) * BR
    reg_end = jnp.cumsum(reg)
    reg_start = reg_end - reg
    goff = reg_start[None, :] + jnp.cumsum(pc, axis=0) - pc
    lend = jnp.cumsum(pc, axis=1)
    lstart = lend - pc
    nbig_run = (pc // BIG)[:, :, None]
    jb = jnp.arange(TS // BIG, dtype=jnp.int32)[None, None, :]
    js = jnp.arange(BIG // RUN_ALIGN - 1, dtype=jnp.int32)[None, None, :]
    entry = (goff << LOCAL_BITS | lstart)[:, :, None]
    step = (1 << LOCAL_BITS) + 1
    big, nbig = _compact(jb < nbig_run, entry + jb * (BIG * step), NBIG)
    small, nsmall = _compact(js < ((pc % BIG) // RUN_ALIGN)[:, :, None],
                             entry + (nbig_run * BIG + js * RUN_ALIGN) * step, NSMALL)
    cnts = jnp.concatenate([nbig, nsmall, lend[:, -1]]).astype(jnp.int32)
    zs = reg_start + tot
    zhead = jnp.minimum(((zs + ZBIG - 1) // ZBIG) * ZBIG, reg_end)
    zrow_s = zs[:, None] + jnp.arange(ZBIG // RUN_ALIGN - 1, dtype=jnp.int32)[None, :] * RUN_ALIGN
    zrow_b = zhead[:, None] + jnp.arange(BR // ZBIG, dtype=jnp.int32)[None, :] * ZBIG
    zsmall, nzs = _compact((zrow_s < zhead[:, None])[None], zrow_s[None], zrow_s.size)
    zbig, nzb = _compact((zrow_b < reg_end[:, None])[None], zrow_b[None], zrow_b.size)
    nb = cap // BR
    blk = jnp.arange(nb, dtype=jnp.int32)
    block_e = jnp.minimum(jnp.sum(reg_end[None, :] <= (blk * BR)[:, None], axis=-1), N_EXPERTS - 1).astype(jnp.int32)
    nvalid = (reg_end[-1] // BR).astype(jnp.int32).reshape(1)
    first = jnp.concatenate([jnp.ones((1,), jnp.int32), (block_e[1:] != block_e[:-1]).astype(jnp.int32)])
    wslot = ((jnp.cumsum(first) - 1) % 2).astype(jnp.int32)
    eid = jnp.arange(N_EXPERTS, dtype=jnp.int32)
    later_used = (eid[None, :] > eid[:, None]) & (reg[None, :] > 0)
    nxt = jnp.min(jnp.where(later_used, eid[None, :], N_EXPERTS), axis=1)
    nxt = jnp.where(nxt < N_EXPERTS, nxt, -1)
    next_e = jnp.sum(jnp.where(block_e[:, None] == eid[None, :], nxt[None, :], 0), axis=1).astype(jnp.int32)
    return dict(big=big, small=small, cnts=cnts, zbig=zbig, zsmall=zsmall,
                zc=jnp.concatenate([nzs, nzb, nvalid]).astype(jnp.int32),
                block_e=block_e, first=first, wslot=wslot, next_e=next_e, nvalid=nvalid)


def kernel(x, norm_mix_g, w_in, b_in, conv_w, conv_b, conv_ln_g, conv_ln_b, w_conv_out, b_conv_out, sg_ln_g, sg_ln_b, w_sg, b_sg, w_sg_out, b_sg_out, w_out, b_out, norm_ffn_g, w_router_group, b_router_group, w_router_expert, b_router_expert, w_expert_gate, w_expert_up, w_expert_down, norm_final_g):
    bsz, seq, d = x.shape
    assert d == D_MODEL and seq % TS == 0 and w_in.shape[0] == 1 and (bsz * seq // TS) % TPS == 0
    nt = bsz * seq // TS
    n_tok = bsz * seq
    cap = -(-(2 * n_tok + nt * N_EXPERTS * (RUN_ALIGN - 1) + N_EXPERTS * (BR - RUN_ALIGN)) // BR) * BR

    row = lambda v: v.reshape(1, -1).astype(F32)
    wr = jnp.zeros((ROUTER_ROWS, d), F32)
    wr = wr.at[0:N_EXPERTS].set(w_router_expert[0].T).at[N_EXPERTS:N_EXPERTS + N_GROUPS].set(w_router_group[0].T)
    br = jnp.zeros((ROUTER_ROWS, 1), F32)
    br = br.at[0:N_EXPERTS, 0].set(b_router_expert[0]).at[N_EXPERTS:N_EXPERTS + N_GROUPS, 0].set(b_router_group[0])
    consts = [
        row(norm_mix_g[0]), w_in[0].astype(BF16), row(b_in[0]),
        conv_w[0], row(conv_b[0]), row(conv_ln_g[0]), row(conv_ln_b[0]),
        w_conv_out[0].astype(BF16), row(b_conv_out[0]),
        row(sg_ln_g[0]), row(sg_ln_b[0]), w_sg[0], b_sg[0].T,
        w_sg_out[0].astype(BF16), row(b_sg_out[0]),
        w_out[0].astype(BF16), row(b_out[0]), row(norm_ffn_g[0]),
        wr.astype(BF16), br,
        (jnp.arange(TS)[:, None] < jnp.arange(TS)[None, :]).astype(BF16),
    ]
    x1, h2, rt, cnt = _mixer(x, consts)
    tabs = _run_tables(cnt, cap)
    xpad = _dispatch(tabs, rt, h2.reshape(n_tok, d), cap)
    ypad = _experts(tabs, xpad, w_expert_gate[0], w_expert_up[0], w_expert_down[0])
    out = _combine(tabs, rt, x1.reshape(n_tok, d), row(norm_final_g), ypad)
    return out.reshape(bsz, seq, d)
```

```python
import math

import jax
import jax.numpy as jnp
from jax import lax
from jax.experimental import pallas as pl
from jax.experimental.pallas import tpu as pltpu

F32 = jnp.float32
BF16 = jnp.bfloat16

D_MODEL = 1024
CONV_WIDTH = 31
SG_HEADS = 8
SG_BLOCK = 128
SG_CHUNK = 64
N_GROUPS = 4
EXPERTS_PER_GROUP = 8
N_EXPERTS = N_GROUPS * EXPERTS_PER_GROUP
D_EXPERT = D_MODEL // 2
EPS = 1e-6

SUBLANES = 8
LANES = 128
TS = 512
HALO = 32
ROWS = TS
STAT_ROWS = TS
CONV_ROWS = 128
NCOL = 512
RUN_ALIGN = SUBLANES
RL = -(-(2 * TS + N_EXPERTS * (RUN_ALIGN - 1)) // LANES) * LANES
NCH = RL // RUN_ALIGN
BIG = 32
NBIG = RL // BIG
NSMALL = N_EXPERTS * (BIG // RUN_ALIGN - 1)
LOCAL_BITS = 11
TPS = 2
BR = 1024
ZBIG = 64
ROUTER_ROWS = 64
NEG = -1e30
assert RL <= 1 << LOCAL_BITS
VMEM_LIMIT = 60 * 1024 * 1024


def _rsqrt_mean_sq(xf):
    return lax.rsqrt(jnp.mean(xf * xf, axis=-1, keepdims=True) + EPS)


def _layer_norm(xf, g, b):
    mu = jnp.mean(xf, axis=-1, keepdims=True)
    xc = xf - mu
    var = jnp.mean(xc * xc, axis=-1, keepdims=True)
    return xc * lax.rsqrt(var + EPS) * g + b


def _sigmoid(z):
    return 0.5 * jnp.tanh(0.5 * z) + 0.5


def _gelu(z):
    return 0.5 * z * (1.0 + lax.erf(z * (0.5 ** 0.5)))


def _rows(i, n):
    return pl.ds(pl.multiple_of(i * n, n), n)


def _matmul_cols(a_ref, w_ref, c0, n, out_ref):
    for j in range(0, n, NCOL):
        out_ref[:, j:j + NCOL] = jnp.dot(a_ref[...], w_ref[:, c0 + j:c0 + j + NCOL],
                                         preferred_element_type=F32)


def _mixer_kernel(x_ref, gmix_ref, win_ref, bin_ref, cw_ref, cb_ref, clg_ref, clb_ref, wco_ref, bco_ref,
                  slg_ref, slb_ref, wsg_ref, bsgt_ref, wso_ref, bso_ref, wout_ref, bout_ref, gffn_ref,
                  wr_ref, br_ref, before_ref,
                  x1_ref, h2_ref, rt_ref, cnt_ref,
                  h_s, p_s, pz_s, pg_s, a_s, c_s, ya_s, yb_s, u_s, v_s):
    d = D_MODEL
    s_idx = pl.program_id(1)

    nlt = d // LANES

    @pl.when(s_idx == 0)
    def _():
        a_s[:, 0:HALO, :] = jnp.zeros((nlt, HALO, LANES), F32)

    @pl.when(s_idx > 0)
    def _():
        a_s[:, 0:HALO, :] = a_s[:, TS:TS + HALO, :]

    def norm_body(i, carry):
        r = _rows(i, STAT_ROWS)
        xf = x_ref[r, :]
        h_s[r, :] = (xf * _rsqrt_mean_sq(xf) * gmix_ref[...]).astype(BF16)
        return carry
    lax.fori_loop(0, TS // STAT_ROWS, norm_body, 0)

    _matmul_cols(h_s, win_ref, 0, 2 * d, p_s)

    def glu_body(i, carry):
        r = _rows(i, ROWS)
        a1 = p_s[r, 0:d] + bin_ref[:, 0:d]
        a2 = p_s[r, d:2 * d] + bin_ref[:, d:2 * d]
        a = a1 * _sigmoid(a2)
        ra = pl.ds(pl.multiple_of(i * ROWS + HALO, math.gcd(ROWS, HALO)), ROWS)
        for c in range(nlt):
            a_s[c, ra, :] = a[:, c * LANES:(c + 1) * LANES]
        return carry
    lax.fori_loop(0, TS // ROWS, glu_body, 0)

    base_off = HALO - (CONV_WIDTH - 1)

    _matmul_cols(h_s, win_ref, 2 * d, 2 * d, pz_s)
    _matmul_cols(h_s, win_ref, 4 * d, 2 * d, pg_s)
    for r0 in range(0, TS, CONV_ROWS):
        for c in range(nlt):
            cs = slice(c * LANES, (c + 1) * LANES)
            acc = jnp.broadcast_to(cb_ref[:, cs], (CONV_ROWS, LANES))
            for k in range(CONV_WIDTH):
                acc = acc + cw_ref[k:k + 1, cs] * a_s[c, r0 + base_off + k:r0 + base_off + k + CONV_ROWS, :]
            p_s[r0:r0 + CONV_ROWS, cs] = acc

    def conv_ln_body(i, carry):
        r = _rows(i, STAT_ROWS)
        y = _layer_norm(p_s[r, 0:d], clg_ref[...], clb_ref[...])
        c_s[r, :] = (y * _sigmoid(y)).astype(BF16)
        return carry
    lax.fori_loop(0, TS // STAT_ROWS, conv_ln_body, 0)

    _matmul_cols(c_s, wco_ref, 0, d, ya_s)


    def gelu_u_body(i, carry):
        r = _rows(i, ROWS)
        u_s[r, :] = _gelu(pz_s[r, 0:d] + bin_ref[:, 2 * d:3 * d])
        return carry
    lax.fori_loop(0, TS // ROWS, gelu_u_body, 0)

    def gelu_v_body(i, carry):
        r = _rows(i, STAT_ROWS)
        z2 = pz_s[r, d:2 * d] + bin_ref[:, 3 * d:4 * d]
        v_s[r, :] = _layer_norm(_gelu(z2), slg_ref[...], slb_ref[...]).astype(BF16)
        return carry
    lax.fori_loop(0, TS // STAT_ROWS, gelu_v_body, 0)

    qi = lax.broadcasted_iota(jnp.int32, (SG_BLOCK, SG_BLOCK), 0) // SG_CHUNK
    ki = lax.broadcasted_iota(jnp.int32, (SG_BLOCK, SG_BLOCK), 1) // SG_CHUNK
    causal = qi >= ki
    for hh in range(SG_HEADS):
        wm = jnp.where(causal, wsg_ref[hh], 0.0).astype(BF16)
        cs = slice(hh * SG_BLOCK, (hh + 1) * SG_BLOCK)
        for blk in range(TS // SG_BLOCK):
            rs = slice(blk * SG_BLOCK, (blk + 1) * SG_BLOCK)
            sv = jnp.dot(wm, v_s[rs, cs], preferred_element_type=F32) + bsgt_ref[:, hh:hh + 1]
            c_s[rs, cs] = (u_s[rs, cs] * sv).astype(BF16)

    _matmul_cols(c_s, wso_ref, 0, d, yb_s)


    def gate_body(i, carry):
        r = _rows(i, ROWS)
        ga = _sigmoid(pg_s[r, 0:d] + bin_ref[:, 4 * d:5 * d])
        gb = _sigmoid(pg_s[r, d:2 * d] + bin_ref[:, 5 * d:6 * d])
        m = ga * (ya_s[r, :] + bco_ref[...]) + gb * (yb_s[r, :] + bso_ref[...])
        c_s[r, :] = m.astype(BF16)
        return carry
    lax.fori_loop(0, TS // ROWS, gate_body, 0)

    _matmul_cols(c_s, wout_ref, 0, d, ya_s)

    def resid_body(i, carry):
        r = _rows(i, STAT_ROWS)
        x1 = x_ref[r, :] + ya_s[r, :] + bout_ref[...]
        x1_ref[r, :] = x1
        h2_ref[r, :] = (x1 * _rsqrt_mean_sq(x1) * gffn_ref[...]).astype(BF16)
        return carry
    lax.fori_loop(0, TS // STAT_ROWS, resid_body, 0)

    lt = lax.dot_general(wr_ref[...], h2_ref[...], (((1,), (1,)), ((), ())),
                         preferred_element_type=F32) + br_ref[...]
    row8 = lax.broadcasted_iota(jnp.int32, (EXPERTS_PER_GROUP, TS), 0)
    gl = jnp.where(row8 < N_GROUPS, lt[N_EXPERTS:N_EXPERTS + 8], NEG)
    gmax = jnp.max(gl, axis=0, keepdims=True)
    gsel = jnp.min(jnp.where(gl == gmax, row8, 8), axis=0, keepdims=True)
    gw = 1.0 / jnp.sum(jnp.exp(gl - gmax), axis=0, keepdims=True)
    el = jnp.zeros((EXPERTS_PER_GROUP, TS), F32)
    for g in range(N_GROUPS):
        el = jnp.where(gsel == g, lt[g * EXPERTS_PER_GROUP:(g + 1) * EXPERTS_PER_GROUP], el)
    m1 = jnp.max(el, axis=0, keepdims=True)
    i1 = jnp.min(jnp.where(el == m1, row8, 8), axis=0, keepdims=True)
    el2 = jnp.where(row8 == i1, NEG, el)
    m2 = jnp.max(el2, axis=0, keepdims=True)
    i2 = jnp.min(jnp.where(el2 == m2, row8, 8), axis=0, keepdims=True)
    t = jnp.exp(m2 - m1)
    w0 = gw / (1.0 + t)
    w1 = gw * t / (1.0 + t)
    e0 = gsel * EXPERTS_PER_GROUP + i1
    e1 = gsel * EXPERTS_PER_GROUP + i2

    row32 = lax.broadcasted_iota(jnp.int32, (N_EXPERTS, TS), 0)
    oh0 = (row32 == e0).astype(F32)
    oh1 = (row32 == e1).astype(F32)
    c0 = jnp.sum(oh0, axis=1, keepdims=True)
    c1 = jnp.sum(oh1, axis=1, keepdims=True)
    padded = jnp.floor((c0 + c1 + (RUN_ALIGN - 1)) * (1.0 / RUN_ALIGN)) * RUN_ALIGN
    tri = (lax.broadcasted_iota(jnp.int32, (N_EXPERTS, N_EXPERTS), 0)
           > lax.broadcasted_iota(jnp.int32, (N_EXPERTS, N_EXPERTS), 1)).astype(BF16)
    units = jnp.broadcast_to(padded * (1.0 / RUN_ALIGN), (N_EXPERTS, LANES)).astype(BF16)
    lstart = RUN_ALIGN * jnp.dot(tri, units, preferred_element_type=F32)[:, 0:1]
    rank = jnp.dot(jnp.concatenate([oh0, oh1], axis=0).astype(BF16), before_ref[...],
                   preferred_element_type=F32)
    pos0 = jnp.sum(oh0 * (lstart + rank[0:N_EXPERTS]), axis=0, keepdims=True)
    pos1 = jnp.sum(oh1 * (lstart + c0 + rank[N_EXPERTS:]), axis=0, keepdims=True)
    rt_ref[...] = jnp.concatenate(
        [pos0, pos1, w0, w1, e0.astype(F32), e1.astype(F32), jnp.zeros((2, TS), F32)], axis=0)
    cnt_ref[...] = jnp.broadcast_to(padded, (N_EXPERTS, LANES))


def _const_spec(shape):
    nd = len(shape)
    return pl.BlockSpec(shape, lambda b, s: (0,) * nd, pipeline_mode=pl.Buffered(1))


def _mixer(x, consts):
    bsz, seq, d = x.shape
    ns = seq // TS
    nt = bsz * ns
    in_specs = [pl.BlockSpec((None, TS, d), lambda b, s: (b, s, 0))]
    in_specs += [_const_spec(c.shape) for c in consts]
    out_shape = (
        jax.ShapeDtypeStruct((bsz, seq, d), F32),
        jax.ShapeDtypeStruct((bsz, seq, d), BF16),
        jax.ShapeDtypeStruct((nt, SUBLANES, TS), F32),
        jax.ShapeDtypeStruct((nt, N_EXPERTS, LANES), F32),
    )
    out_specs = (
        pl.BlockSpec((None, TS, d), lambda b, s: (b, s, 0)),
        pl.BlockSpec((None, TS, d), lambda b, s: (b, s, 0)),
        pl.BlockSpec((None, SUBLANES, TS), lambda b, s: (b * ns + s, 0, 0)),
        pl.BlockSpec((None, N_EXPERTS, LANES), lambda b, s: (b * ns + s, 0, 0)),
    )
    scratch = [
        pltpu.VMEM((TS, d), BF16),
        pltpu.VMEM((TS, 2 * d), F32),
        pltpu.VMEM((TS, 2 * d), F32),
        pltpu.VMEM((TS, 2 * d), F32),
        pltpu.VMEM((d // LANES, TS + HALO, LANES), F32),
        pltpu.VMEM((TS, d), BF16),
        pltpu.VMEM((TS, d), F32),
        pltpu.VMEM((TS, d), F32),
        pltpu.VMEM((TS, d), F32),
        pltpu.VMEM((TS, d), BF16),
    ]
    return pl.pallas_call(
        _mixer_kernel,
        grid=(bsz, ns),
        in_specs=in_specs,
        out_specs=out_specs,
        out_shape=out_shape,
        scratch_shapes=scratch,
        compiler_params=pltpu.CompilerParams(
            dimension_semantics=("arbitrary", "arbitrary"), vmem_limit_bytes=VMEM_LIMIT),
        name="mixer",
    )(x, *consts)


def _run_copy(src, dst, sem, s0, d0, rows):
    return pltpu.make_async_copy(src.at[pl.ds(pl.multiple_of(s0, RUN_ALIGN), rows)],
                                 dst.at[pl.ds(pl.multiple_of(d0, RUN_ALIGN), rows)], sem)


def _start_run_copies(big_ref, small_ref, nbig, nsmall, t, vmem_buf, hbm_ref, sem, to_vmem):
    def start(entry, rows):
        local = entry & ((1 << LOCAL_BITS) - 1)
        glob = lax.shift_right_logical(entry, LOCAL_BITS)
        if to_vmem:
            _run_copy(hbm_ref, vmem_buf, sem, glob, local, rows).start()
        else:
            _run_copy(vmem_buf, hbm_ref, sem, local, glob, rows).start()

    def big_body(j, carry):
        start(big_ref[t * NBIG + j], BIG)
        return carry
    lax.fori_loop(0, nbig, big_body, 0)

    def small_body(j, carry):
        start(small_ref[t * NSMALL + j], RUN_ALIGN)
        return carry
    lax.fori_loop(0, nsmall, small_body, 0)


def _wait_run_copies(vmem_buf, hbm_ref, sem, rows, to_vmem):
    sl = pl.ds(0, pl.multiple_of(rows, RUN_ALIGN))
    if to_vmem:
        pltpu.make_async_copy(hbm_ref.at[sl], vmem_buf.at[sl], sem).wait()
    else:
        pltpu.make_async_copy(vmem_buf.at[sl], hbm_ref.at[sl], sem).wait()


def _dispatch_kernel(big_ref, small_ref, cnts_ref, zbig_ref, zsmall_ref, zc_ref, rt_ref, h2_ref, xpad_ref,
                     buf, zbuf, sem, zsem):
    i = pl.program_id(0)
    nsteps = pl.num_programs(0)
    nt = nsteps * TPS
    base = (i % 2) * TPS
    slot = lax.broadcasted_iota(jnp.int32, (RL, TS), 0).astype(F32)
    for k in range(TPS):
        t = i * TPS + k
        pos0 = rt_ref[k, 0:1, :]
        pos1 = rt_ref[k, 1:2, :]
        onehot = ((slot == pos0) | (slot == pos1)).astype(BF16)
        buf[base + k] = jnp.dot(onehot, h2_ref[k * TS:(k + 1) * TS, :], preferred_element_type=F32)
        _start_run_copies(big_ref, small_ref, cnts_ref[t], cnts_ref[nt + t], t, buf.at[base + k], xpad_ref,
                          sem.at[base + k], to_vmem=False)

    def wait_step(step, first_buf):
        for k in range(TPS):
            _wait_run_copies(buf.at[first_buf + k], xpad_ref, sem.at[first_buf + k],
                             cnts_ref[2 * nt + step * TPS + k], to_vmem=False)

    @pl.when(i > 0)
    def _():
        wait_step(i - 1, TPS - base)

    @pl.when(i == nsteps - 1)
    def _():
        zbuf[...] = jnp.zeros((BR, D_MODEL), F32)
        nb = xpad_ref.shape[0] // BR

        def zcopy(row, rows):
            return pltpu.make_async_copy(zbuf.at[pl.ds(0, rows)],
                                         xpad_ref.at[pl.ds(pl.multiple_of(row, RUN_ALIGN), rows)], zsem)

        def fill(tab_ref, count, rows):
            def issue(j, carry):
                zcopy(tab_ref[j], rows).start()
                return carry
            lax.fori_loop(0, count, issue, 0)

            def wait(j, carry):
                zcopy(0, rows).wait()
                return carry
            lax.fori_loop(0, count, wait, 0)

        fill(zsmall_ref, zc_ref[0], RUN_ALIGN)
        fill(zbig_ref, zc_ref[1], ZBIG)

        def bissue(b, carry):
            zcopy(b * BR, BR).start()
            return carry
        lax.fori_loop(zc_ref[2], nb, bissue, 0)

        def bwait(b, carry):
            zcopy(0, BR).wait()
            return carry
        lax.fori_loop(zc_ref[2], nb, bwait, 0)

        wait_step(i, base)


def _dispatch(tabs, rt, h2, cap):
    nt = rt.shape[0]
    d = h2.shape[-1]
    grid_spec = pltpu.PrefetchScalarGridSpec(
        num_scalar_prefetch=6,
        grid=(nt // TPS,),
        in_specs=[
            pl.BlockSpec((TPS, SUBLANES, TS), lambda i, *_: (i, 0, 0)),
            pl.BlockSpec((TPS * TS, d), lambda i, *_: (i, 0)),
        ],
        out_specs=pl.BlockSpec(memory_space=pl.ANY),
        scratch_shapes=[
            pltpu.VMEM((2 * TPS, RL, d), F32),
            pltpu.VMEM((BR, d), F32),
            pltpu.SemaphoreType.DMA((2 * TPS,)),
            pltpu.SemaphoreType.DMA,
        ],
    )
    return pl.pallas_call(
        _dispatch_kernel,
        grid_spec=grid_spec,
        out_shape=jax.ShapeDtypeStruct((cap, d), F32),
        compiler_params=pltpu.CompilerParams(
            dimension_semantics=("arbitrary",), vmem_limit_bytes=VMEM_LIMIT),
        name="dispatch",
    )(tabs["big"], tabs["small"], tabs["cnts"], tabs["zbig"], tabs["zsmall"], tabs["zc"], rt, h2)


def _expert_kernel(be_ref, first_ref, wslot_ref, nexte_ref, nv_ref, x_ref, wg_hbm, wu_hbm, wd_hbm, y_ref,
                   wg_f, wu_f, wd_f, wg_s, wu_s, wd_s, sem):
    b = pl.program_id(0)

    def weight_copies(e, ws):
        return (pltpu.make_async_copy(wg_hbm.at[e], wg_f.at[ws], sem.at[0, ws]),
                pltpu.make_async_copy(wu_hbm.at[e], wu_f.at[ws], sem.at[1, ws]),
                pltpu.make_async_copy(wd_hbm.at[e], wd_f.at[ws], sem.at[2, ws]))

    @pl.when((b < nv_ref[0]) & (first_ref[b] == 1))
    def _():
        e = be_ref[b]
        ws = wslot_ref[b]
        ne = nexte_ref[b]

        @pl.when(b == 0)
        def _():
            for c in weight_copies(e, ws):
                c.start()

        @pl.when(ne >= 0)
        def _():
            for c in weight_copies(ne, 1 - ws):
                c.start()

        for c in weight_copies(e, ws):
            c.wait()
        wg_s[...] = wg_f[ws].astype(BF16)
        wu_s[...] = wu_f[ws].astype(BF16)
        wd_s[...] = wd_f[ws].astype(BF16)

    @pl.when(b < nv_ref[0])
    def _():
        xb = x_ref[...].astype(BF16)
        g = jnp.dot(xb, wg_s[...], preferred_element_type=F32)
        u = jnp.dot(xb, wu_s[...], preferred_element_type=F32)
        hid = (g * _sigmoid(g) * u).astype(BF16)
        y_ref[...] = jnp.dot(hid, wd_s[...], preferred_element_type=F32)


def _experts(tabs, xpad, wg, wu, wd):
    cap, d = xpad.shape
    nb = cap // BR

    def row_map(b, be, fi, ws, ne, nv):
        return (jnp.minimum(b, nv[0] - 1), 0)

    grid_spec = pltpu.PrefetchScalarGridSpec(
        num_scalar_prefetch=5,
        grid=(nb,),
        in_specs=[
            pl.BlockSpec((BR, d), row_map),
            pl.BlockSpec(memory_space=pl.ANY),
            pl.BlockSpec(memory_space=pl.ANY),
            pl.BlockSpec(memory_space=pl.ANY),
        ],
        out_specs=pl.BlockSpec((BR, d), row_map),
        scratch_shapes=[
            pltpu.VMEM((2, d, D_EXPERT), F32),
            pltpu.VMEM((2, d, D_EXPERT), F32),
            pltpu.VMEM((2, D_EXPERT, d), F32),
            pltpu.VMEM((d, D_EXPERT), BF16),
            pltpu.VMEM((d, D_EXPERT), BF16),
            pltpu.VMEM((D_EXPERT, d), BF16),
            pltpu.SemaphoreType.DMA((3, 2)),
        ],
    )
    return pl.pallas_call(
        _expert_kernel,
        grid_spec=grid_spec,
        out_shape=jax.ShapeDtypeStruct((cap, d), F32),
        input_output_aliases={5: 0},
        compiler_params=pltpu.CompilerParams(
            dimension_semantics=("arbitrary",), vmem_limit_bytes=VMEM_LIMIT),
        name="experts",
    )(tabs["block_e"], tabs["first"], tabs["wslot"], tabs["next_e"], tabs["nvalid"], xpad, wg, wu, wd)


def _combine_kernel(big_ref, small_ref, cnts_ref, rt_ref, x1_ref, gfin_ref, ypad_ref, out_ref, buf, sem):
    i = pl.program_id(0)
    nsteps = pl.num_programs(0)
    nt = nsteps * TPS
    base = (i % 2) * TPS

    def fetch_step(step, first_buf):
        for k in range(TPS):
            t = step * TPS + k
            b = first_buf + k
            _start_run_copies(big_ref, small_ref, cnts_ref[t], cnts_ref[nt + t], t, buf.at[b], ypad_ref,
                              sem.at[b], to_vmem=True)

            def zero_body(j, carry, b=b):
                buf[b, pl.ds(pl.multiple_of(j * RUN_ALIGN, RUN_ALIGN), RUN_ALIGN), :] = jnp.zeros(
                    (RUN_ALIGN, D_MODEL), F32)
                return carry
            lax.fori_loop(cnts_ref[2 * nt + t] // RUN_ALIGN, NCH, zero_body, 0)

    @pl.when(i == 0)
    def _():
        fetch_step(0, 0)

    @pl.when(i < nsteps - 1)
    def _():
        fetch_step(i + 1, TPS - base)

    lane_slot = lax.broadcasted_iota(jnp.int32, (TS, RL), 1).astype(F32)
    for k in range(TPS):
        _wait_run_copies(buf.at[base + k], ypad_ref, sem.at[base + k], cnts_ref[2 * nt + i * TPS + k], to_vmem=True)
    unsorts = []
    for k in range(TPS):
        rt_cols = jnp.concatenate([rt_ref[k], jnp.zeros((LANES - SUBLANES, TS), F32)], axis=0).T
        p0c = rt_cols[:, 0:1]
        p1c = rt_cols[:, 1:2]
        w0c = rt_cols[:, 2:3]
        w1c = rt_cols[:, 3:4]
        unsorts.append(jnp.where(lane_slot == p0c, w0c, jnp.where(lane_slot == p1c, w1c, 0.0)).astype(BF16))
    moes = [jnp.dot(unsorts[k], buf[base + k].astype(BF16), preferred_element_type=F32) for k in range(TPS)]
    for k in range(TPS):
        rows = slice(k * TS, (k + 1) * TS)
        x2 = x1_ref[rows, :] + moes[k]
        out_ref[rows, :] = x2 * _rsqrt_mean_sq(x2) * gfin_ref[...]


def _combine(tabs, rt, x1, gfin, ypad):
    nt = rt.shape[0]
    d = x1.shape[-1]
    grid_spec = pltpu.PrefetchScalarGridSpec(
        num_scalar_prefetch=3,
        grid=(nt // TPS,),
        in_specs=[
            pl.BlockSpec((TPS, SUBLANES, TS), lambda i, *_: (i, 0, 0)),
            pl.BlockSpec((TPS * TS, d), lambda i, *_: (i, 0)),
            pl.BlockSpec((1, d), lambda i, *_: (0, 0)),
            pl.BlockSpec(memory_space=pl.ANY),
        ],
        out_specs=pl.BlockSpec((TPS * TS, d), lambda i, *_: (i, 0)),
        scratch_shapes=[pltpu.VMEM((2 * TPS, RL, d), F32), pltpu.SemaphoreType.DMA((2 * TPS,))],
    )
    return pl.pallas_call(
        _combine_kernel,
        grid_spec=grid_spec,
        out_shape=jax.ShapeDtypeStruct((nt * TS, d), F32),
        compiler_params=pltpu.CompilerParams(
            dimension_semantics=("arbitrary",), vmem_limit_bytes=VMEM_LIMIT),
        name="combine",
    )(tabs["big"], tabs["small"], tabs["cnts"], rt, x1, gfin, ypad)


def _compact(valid, values, width):
    v = valid.reshape(valid.shape[0], -1)
    _, out = lax.sort((~v, values.reshape(v.shape)), dimension=1, is_stable=True, num_keys=1)
    return out[:, :width].astype(jnp.int32).reshape(-1), jnp.sum(v, axis=1).astype(jnp.int32)


def _run_tables(cnt, cap):
    pc = cnt[:, :, 0].astype(jnp.int32)
    tot = jnp.sum(pc, axis=0)
    reg = ((tot + BR - 1) // BR) * BR
    reg_end = jnp.cumsum(reg)
    reg_start = reg_end - reg
    goff = reg_start[None, :] + jnp.cumsum(pc, axis=0) - pc
    lend = jnp.cumsum(pc, axis=1)
    lstart = lend - pc
    nbig_run = (pc // BIG)[:, :, None]
    jb = jnp.arange(TS // BIG, dtype=jnp.int32)[None, None, :]
    js = jnp.arange(BIG // RUN_ALIGN - 1, dtype=jnp.int32)[None, None, :]
    entry = (goff << LOCAL_BITS | lstart)[:, :, None]
    step = (1 << LOCAL_BITS) + 1
    big, nbig = _compact(jb < nbig_run, entry + jb * (BIG * step), NBIG)
    small, nsmall = _compact(js < ((pc % BIG) // RUN_ALIGN)[:, :, None],
                             entry + (nbig_run * BIG + js * RUN_ALIGN) * step, NSMALL)
    cnts = jnp.concatenate([nbig, nsmall, lend[:, -1]]).astype(jnp.int32)
    zs = reg_start + tot
    zhead = jnp.minimum(((zs + ZBIG - 1) // ZBIG) * ZBIG, reg_end)
    zrow_s = zs[:, None] + jnp.arange(ZBIG // RUN_ALIGN - 1, dtype=jnp.int32)[None, :] * RUN_ALIGN
    zrow_b = zhead[:, None] + jnp.arange(BR // ZBIG, dtype=jnp.int32)[None, :] * ZBIG
    zsmall, nzs = _compact((zrow_s < zhead[:, None])[None], zrow_s[None], zrow_s.size)
    zbig, nzb = _compact((zrow_b < reg_end[:, None])[None], zrow_b[None], zrow_b.size)
    nb = cap // BR
    blk = jnp.arange(nb, dtype=jnp.int32)
    block_e = jnp.minimum(jnp.sum(reg_end[None, :] <= (blk * BR)[:, None], axis=-1), N_EXPERTS - 1).astype(jnp.int32)
    nvalid = (reg_end[-1] // BR).astype(jnp.int32).reshape(1)
    first = jnp.concatenate([jnp.ones((1,), jnp.int32), (block_e[1:] != block_e[:-1]).astype(jnp.int32)])
    wslot = ((jnp.cumsum(first) - 1) % 2).astype(jnp.int32)
    eid = jnp.arange(N_EXPERTS, dtype=jnp.int32)
    later_used = (eid[None, :] > eid[:, None]) & (reg[None, :] > 0)
    nxt = jnp.min(jnp.where(later_used, eid[None, :], N_EXPERTS), axis=1)
    nxt = jnp.where(nxt < N_EXPERTS, nxt, -1)
    next_e = jnp.sum(jnp.where(block_e[:, None] == eid[None, :], nxt[None, :], 0), axis=1).astype(jnp.int32)
    return dict(big=big, small=small, cnts=cnts, zbig=zbig, zsmall=zsmall,
                zc=jnp.concatenate([nzs, nzb, nvalid]).astype(jnp.int32),
                block_e=block_e, first=first, wslot=wslot, next_e=next_e, nvalid=nvalid)


def kernel(x, norm_mix_g, w_in, b_in, conv_w, conv_b, conv_ln_g, conv_ln_b, w_conv_out, b_conv_out, sg_ln_g, sg_ln_b, w_sg, b_sg, w_sg_out, b_sg_out, w_out, b_out, norm_ffn_g, w_router_group, b_router_group, w_router_expert, b_router_expert, w_expert_gate, w_expert_up, w_expert_down, norm_final_g):
    bsz, seq, d = x.shape
    assert d == D_MODEL and seq % TS == 0 and w_in.shape[0] == 1 and (bsz * seq // TS) % TPS == 0
    nt = bsz * seq // TS
    n_tok = bsz * seq
    cap = -(-(2 * n_tok + nt * N_EXPERTS * (RUN_ALIGN - 1) + N_EXPERTS * (BR - RUN_ALIGN)) // BR) * BR

    row = lambda v: v.reshape(1, -1).astype(F32)
    wr = jnp.zeros((ROUTER_ROWS, d), F32)
    wr = wr.at[0:N_EXPERTS].set(w_router_expert[0].T).at[N_EXPERTS:N_EXPERTS + N_GROUPS].set(w_router_group[0].T)
    br = jnp.zeros((ROUTER_ROWS, 1), F32)
    br = br.at[0:N_EXPERTS, 0].set(b_router_expert[0]).at[N_EXPERTS:N_EXPERTS + N_GROUPS, 0].set(b_router_group[0])
    consts = [
        row(norm_mix_g[0]), w_in[0].astype(BF16), row(b_in[0]),
        conv_w[0], row(conv_b[0]), row(conv_ln_g[0]), row(conv_ln_b[0]),
        w_conv_out[0].astype(BF16), row(b_conv_out[0]),
        row(sg_ln_g[0]), row(sg_ln_b[0]), w_sg[0], b_sg[0].T,
        w_sg_out[0].astype(BF16), row(b_sg_out[0]),
        w_out[0].astype(BF16), row(b_out[0]), row(norm_ffn_g[0]),
        wr.astype(BF16), br,
        (jnp.arange(TS)[:, None] < jnp.arange(TS)[None, :]).astype(BF16),
    ]
    x1, h2, rt, cnt = _mixer(x, consts)
    tabs = _run_tables(cnt, cap)
    xpad = _dispatch(tabs, rt, h2.reshape(n_tok, d), cap)
    ypad = _experts(tabs, xpad, w_expert_gate[0], w_expert_up[0], w_expert_down[0])
    out = _combine(tabs, rt, x1.reshape(n_tok, d), row(norm_final_g), ypad)
    return out.reshape(bsz, seq, d)
```
